```python
import math
import jax, jax.numpy as jnp
from jax import lax
import numpy as np

D_MODEL = 1024
BATCH = 4
SEQ = 4096
DEPTH = 1

N_META = 16
ROPE_THETA = 500000.0
NORM_EPS = 1e-5

DA_HEADS = 8
DA_HEAD_DIM = 64
DA_V_DIM = 2 * DA_HEAD_DIM
DA_QK_WIDTH = DA_HEADS * 2 * DA_HEAD_DIM
DA_WIDTH = DA_HEADS * DA_V_DIM
DA_ROT_DIM = DA_HEAD_DIM // 4
Q_BLOCK = 128

GLA_HEADS = 4
GLA_KEY_DIM = D_MODEL // 2
GLA_VAL_DIM = D_MODEL
GLA_DK = GLA_KEY_DIM // GLA_HEADS
GLA_DV = GLA_VAL_DIM // GLA_HEADS
GLA_GATE_RANK = 16
GLA_GATE_NORMALIZER = 16.0
GLA_CHUNK = 64

IN_SPLITS = (DA_QK_WIDTH, DA_QK_WIDTH, DA_WIDTH, DA_WIDTH,
             GLA_KEY_DIM, GLA_KEY_DIM, GLA_VAL_DIM, GLA_VAL_DIM,
             GLA_GATE_RANK,
             D_MODEL, D_MODEL)
W_IN_COLS = 4 * 1024 + 512 + 512 + 1024 + 1024 + 16 + 2 * 1024

kernel_name = "hybrid_diffattn_gla_gated_block"


def rms_norm(x, w, eps=NORM_EPS):
    xf = x.astype(jnp.float32)
    y = xf * lax.rsqrt(jnp.mean(xf * xf, axis=-1, keepdims=True) + eps)
    return (y * w.astype(jnp.float32)).astype(x.dtype)


def partial_rope(x, pos):
    half = DA_ROT_DIM // 2
    inv_freq = ROPE_THETA ** (-jnp.arange(half, dtype=jnp.float32) / half)
    ang = pos.astype(jnp.float32)[:, None] * inv_freq[None, :]
    cos, sin = jnp.cos(ang), jnp.sin(ang)
    xr = x[..., :DA_ROT_DIM].astype(jnp.float32)
    x1, x2 = xr[..., :half], xr[..., half:]
    rot = jnp.concatenate([x1 * cos - x2 * sin, x2 * cos + x1 * sin], axis=-1).astype(x.dtype)
    return jnp.concatenate([rot, x[..., DA_ROT_DIM:]], axis=-1)


def diff_attn_block(q1, q2, q_pos, k1, k2, v, k_pos, lam):
    scale = DA_HEAD_DIM ** -0.5
    mask = k_pos[None, :] <= q_pos[:, None]

    def probs(q, k):
        s = jnp.einsum("bhqd,bhkd->bhqk", q, k).astype(jnp.float32) * scale
        return jax.nn.softmax(jnp.where(mask, s, -jnp.inf), axis=-1)

    p = probs(q1, k1) - lam * probs(q2, k2)
    return jnp.einsum("bhqk,bhkv->bhqv", p.astype(v.dtype), v)


def diff_attention(q1, q2, k1, k2, v, pos, lam):
    B, H, L, d = q1.shape
    S = L - N_META
    nb = S // Q_BLOCK
    o_meta = diff_attn_block(q1[:, :, :N_META], q2[:, :, :N_META], pos[:N_META],
                             k1[:, :, :N_META], k2[:, :, :N_META], v[:, :, :N_META],
                             pos[:N_META], lam)

    def blocks(t):
        return t[:, :, N_META:].reshape(B, H, nb, Q_BLOCK, t.shape[-1]).transpose(2, 0, 1, 3, 4)

    pos_b = pos[N_META:].reshape(nb, Q_BLOCK)
    o_real = lax.map(lambda a: diff_attn_block(a[0], a[1], a[2], k1, k2, v, pos, lam),
                     (blocks(q1), blocks(q2), pos_b))
    o_real = o_real.transpose(1, 2, 0, 3, 4).reshape(B, H, S, v.shape[-1])
    return jnp.concatenate([o_meta, o_real], axis=2)


def gla_chunk(state, q, k, v, lg):
    f32 = jnp.float32
    C = q.shape[2]
    qf = q.astype(f32) * (GLA_DK ** -0.5)
    kf, vf = k.astype(f32), v.astype(f32)
    b = jnp.cumsum(lg.astype(f32), axis=2)
    causal = jnp.tril(jnp.ones((C, C), dtype=bool))[None, None, :, :, None]
    decay = jnp.exp(jnp.where(causal, b[:, :, :, None, :] - b[:, :, None, :, :], -jnp.inf))
    a = jnp.einsum("bhtc,bhjc,bhtjc->bhtj", qf, kf, decay)
    o = (jnp.einsum("bhtj,bhjv->bhtv", a, vf)
         + jnp.einsum("bhtc,bhcv->bhtv", qf * jnp.exp(b), state))
    b_last = b[:, :, -1]
    new_state = (jnp.exp(b_last)[..., None] * state
                 + jnp.einsum("bhjc,bhjv->bhcv", kf * jnp.exp(b_last[:, :, None, :] - b), vf))
    return new_state, o.astype(v.dtype)


def gla(q, k, v, lg):
    B, H, L, _ = q.shape
    S = L - N_META
    nc = S // GLA_CHUNK
    s0 = jnp.zeros((B, H, GLA_DK, GLA_DV), jnp.float32)
    s1, o_meta = gla_chunk(s0, q[:, :, :N_META], k[:, :, :N_META], v[:, :, :N_META], lg[:, :, :N_META])

    def chunks(t):
        return t[:, :, N_META:].reshape(B, H, nc, GLA_CHUNK, t.shape[-1]).transpose(2, 0, 1, 3, 4)

    _, o_real = lax.scan(lambda st, xs: gla_chunk(st, *xs), s1,
                         (chunks(q), chunks(k), chunks(v), chunks(lg)))
    o_real = o_real.transpose(1, 2, 0, 3, 4).reshape(B, H, S, GLA_DV)
    return jnp.concatenate([o_meta, o_real], axis=2)


def hybrid_layer(h, pos, lam_init, norm_w, w_in, lam_q1, lam_k1, lam_q2, lam_k2, da_subln_w,
                 gla_gate_w2, gla_gate_b, gla_norm_w, w_branch_a, w_branch_b, w_out):
    B, L, _ = h.shape
    u = rms_norm(h, norm_w)
    proj = u @ w_in
    (a_q, a_k, a_v, a_z, g_q, g_k, g_v, g_z, g_lr, gate_a, gate_b) = jnp.split(
        proj, list(np.cumsum(IN_SPLITS)[:-1]), axis=-1)

    qa = a_q.reshape(B, L, DA_HEADS, 2, DA_HEAD_DIM).transpose(3, 0, 2, 1, 4)
    ka = a_k.reshape(B, L, DA_HEADS, 2, DA_HEAD_DIM).transpose(3, 0, 2, 1, 4)
    va = a_v.reshape(B, L, DA_HEADS, DA_V_DIM).transpose(0, 2, 1, 3)
    q1, q2 = partial_rope(qa[0], pos), partial_rope(qa[1], pos)
    k1, k2 = partial_rope(ka[0], pos), partial_rope(ka[1], pos)
    lam = (jnp.exp(jnp.sum(lam_q1.astype(jnp.float32) * lam_k1.astype(jnp.float32)))
           - jnp.exp(jnp.sum(lam_q2.astype(jnp.float32) * lam_k2.astype(jnp.float32)))
           + lam_init)
    o_a = diff_attention(q1, q2, k1, k2, va, pos, lam)
    o_a = rms_norm(o_a, da_subln_w) * (1.0 - lam_init)
    o_a = o_a.transpose(0, 2, 1, 3).reshape(B, L, DA_WIDTH) * jax.nn.silu(a_z)
    y_a = o_a @ w_branch_a

    qb = g_q.reshape(B, L, GLA_HEADS, GLA_DK).transpose(0, 2, 1, 3)
    kb = g_k.reshape(B, L, GLA_HEADS, GLA_DK).transpose(0, 2, 1, 3)
    vb = g_v.reshape(B, L, GLA_HEADS, GLA_DV).transpose(0, 2, 1, 3)
    gk = (g_lr @ gla_gate_w2 + gla_gate_b).astype(jnp.float32)
    lg = (jax.nn.log_sigmoid(gk) / GLA_GATE_NORMALIZER).reshape(B, L, GLA_HEADS, GLA_DK).transpose(0, 2, 1, 3)
    o_b = rms_norm(gla(qb, kb, vb, lg), gla_norm_w)
    o_b = o_b.transpose(0, 2, 1, 3).reshape(B, L, GLA_VAL_DIM) * jax.nn.silu(g_z)
    y_b = o_b @ w_branch_b

    merged = jax.nn.sigmoid(gate_a) * y_a + jax.nn.sigmoid(gate_b) * y_b
    return h + merged @ w_out


def setup_inputs(seed: int = 0) -> dict:
    key = jax.random.key(seed)
    ks = jax.random.split(key, 17)
    f32 = jnp.float32
    n = lambda k, shape, s: jax.random.normal(k, shape, f32) * s
    return {
        "x": n(ks[0], (BATCH, SEQ, D_MODEL), 1.0),
        "meta_tokens": n(ks[1], (N_META, D_MODEL), 1.0),
        "norm_w": 1.0 + n(ks[2], (DEPTH, D_MODEL), 0.02),
        "w_in": n(ks[3], (DEPTH, D_MODEL, W_IN_COLS), D_MODEL ** -0.5),
        "lam_q1": n(ks[4], (DEPTH, DA_HEAD_DIM), 0.1),
        "lam_k1": n(ks[5], (DEPTH, DA_HEAD_DIM), 0.1),
        "lam_q2": n(ks[6], (DEPTH, DA_HEAD_DIM), 0.1),
        "lam_k2": n(ks[7], (DEPTH, DA_HEAD_DIM), 0.1),
        "da_subln_w": 1.0 + n(ks[8], (DEPTH, DA_V_DIM), 0.02),
        "gla_gate_w2": n(ks[9], (DEPTH, GLA_GATE_RANK, GLA_KEY_DIM), GLA_GATE_RANK ** -0.5),
        "gla_gate_b": n(ks[10], (DEPTH, GLA_KEY_DIM), 0.01),
        "gla_norm_w": 1.0 + n(ks[11], (DEPTH, GLA_DV), 0.02),
        "w_branch_a": n(ks[12], (DEPTH, DA_WIDTH, D_MODEL), DA_WIDTH ** -0.5),
        "w_branch_b": n(ks[13], (DEPTH, GLA_VAL_DIM, D_MODEL), GLA_VAL_DIM ** -0.5),
        "w_out": n(ks[14], (DEPTH, D_MODEL, D_MODEL), D_MODEL ** -0.5),
        "final_norm_w": 1.0 + n(ks[15], (D_MODEL,), 0.02),
    }


def reference(x, meta_tokens, norm_w, w_in, lam_q1, lam_k1, lam_q2, lam_k2, da_subln_w,
              gla_gate_w2, gla_gate_b, gla_norm_w, w_branch_a, w_branch_b, w_out, final_norm_w):
    B, S, D = x.shape
    meta = jnp.broadcast_to(meta_tokens.astype(x.dtype)[None], (B, N_META, D))
    h = jnp.concatenate([meta, x], axis=1)
    pos = jnp.arange(N_META + S, dtype=jnp.int32)
    for layer in range(DEPTH):
        lam_init = 0.8 - 0.6 * math.exp(-0.3 * layer)
        h = hybrid_layer(h, pos, lam_init, norm_w[layer], w_in[layer], lam_q1[layer], lam_k1[layer],
                         lam_q2[layer], lam_k2[layer], da_subln_w[layer], gla_gate_w2[layer],
                         gla_gate_b[layer], gla_norm_w[layer], w_branch_a[layer], w_branch_b[layer],
                         w_out[layer])
    return rms_norm(h, final_norm_w)[:, N_META:]
```

```python
import functools
import math

import numpy as np
import jax
import jax.numpy as jnp
from jax import lax
from jax.experimental import pallas as pl
from jax.experimental.pallas import tpu as pltpu

F32 = jnp.float32
BF16 = jnp.bfloat16

N_META = 16
ROPE_THETA = 500000.0
NORM_EPS = 1e-5

DA_HEADS = 8
DA_HEAD_DIM = 64
DA_V_DIM = 128
DA_ROT_DIM = 16
GLA_HEADS = 4
GLA_DK = 128
GLA_DV = 256
GLA_GATE_RANK = 16
GLA_GATE_NORMALIZER = 16.0
GLA_CHUNK = 64

LANES = 128
VMEM_LIMIT = 48 * 1024 * 1024

P_Q, P_K, P_Z, P_GQ, P_GK, P_GV, P_GZ, P_GA, P_GB = 0, 8, 16, 24, 28, 32, 40, 48, 56
P_COLS = 64 * LANES
PROJ_TN = 512

NEG_BIG = -1e30


def _nt(a, b):
    return lax.dot_general(a, b, (((1,), (1,)), ((), ())), preferred_element_type=F32)


def _tn(a, b):
    return lax.dot_general(a, b, (((0,), (0,)), ((), ())), preferred_element_type=F32)


def _nn(a, b):
    return jnp.dot(a, b, preferred_element_type=F32)


def _params(n_axes):
    return pltpu.CompilerParams(dimension_semantics=("arbitrary",) * n_axes,
                                vmem_limit_bytes=VMEM_LIMIT)


def _proj_kernel(x_ref, nw_ref, w_ref, cos_ref, sa_ref, sb_ref, p_ref, u_ref, u_scr):
    j = pl.program_id(1)

    @pl.when(j == 0)
    def _():
        x = x_ref[...]
        ms = jnp.mean(x * x, axis=-1, keepdims=True)
        u = (x * lax.rsqrt(ms + NORM_EPS) * nw_ref[...]).astype(BF16)
        u_scr[...] = u
        u_ref[...] = u

    def acc():
        return _nn(u_scr[...], w_ref[...])

    def rope(y, scale):
        outs = []
        for g in range(PROJ_TN // LANES):
            t = y[:, g * LANES:(g + 1) * LANES]
            r = (t * cos_ref[...] + pltpu.roll(t, LANES - DA_ROT_DIM // 2, 1) * sa_ref[...]
                 + pltpu.roll(t, DA_ROT_DIM // 2, 1) * sb_ref[...])
            outs.append(r * scale if scale != 1.0 else r)
        return jnp.concatenate(outs, axis=1)

    nq = (P_K - P_Q) * LANES // PROJ_TN
    nk = (P_Z - P_K) * LANES // PROJ_TN
    b_k, b_z, b_gq, b_gk = nq, nq + nk, P_GQ * LANES // PROJ_TN, P_GK * LANES // PROJ_TN
    b_gz, b_ga = P_GZ * LANES // PROJ_TN, P_GA * LANES // PROJ_TN

    @pl.when(j < b_k)
    def _():
        p_ref[...] = rope(acc(), DA_HEAD_DIM ** -0.5).astype(BF16)

    @pl.when((j >= b_k) & (j < b_z))
    def _():
        p_ref[...] = rope(acc(), 1.0).astype(BF16)

    @pl.when(((j >= b_z) & (j < b_gq)) | ((j >= b_gz) & (j < b_ga)))
    def _():
        y = acc()
        p_ref[...] = (y * jax.nn.sigmoid(y)).astype(BF16)

    @pl.when(j == b_gq)
    def _():
        p_ref[...] = (acc() * (GLA_DK ** -0.5)).astype(BF16)

    @pl.when((j >= b_gk) & (j < b_gz))
    def _():
        p_ref[...] = acc().astype(BF16)

    @pl.when(j >= b_ga)
    def _():
        p_ref[...] = jax.nn.sigmoid(acc()).astype(BF16)


def _proj_call(x2, norm_w, w_main, cos_t, sa_t, sb_t, tm):
    rows, d = x2.shape
    n_pos_blocks = cos_t.shape[0] // tm
    grid = (rows // tm, P_COLS // PROJ_TN)
    tab_spec = pl.BlockSpec((tm, LANES), lambda i, j: (i % n_pos_blocks, 0))
    return pl.pallas_call(
        _proj_kernel,
        grid=grid,
        in_specs=[
            pl.BlockSpec((tm, d), lambda i, j: (i, 0)),
            pl.BlockSpec((1, d), lambda i, j: (0, 0)),
            pl.BlockSpec((d, PROJ_TN), lambda i, j: (0, j)),
            tab_spec, tab_spec, tab_spec,
        ],
        out_specs=[
            pl.BlockSpec((tm, PROJ_TN), lambda i, j: (i, j)),
            pl.BlockSpec((tm, d), lambda i, j: (i, 0)),
        ],
        out_shape=[
            jax.ShapeDtypeStruct((rows, P_COLS), BF16),
            jax.ShapeDtypeStruct((rows, d), BF16),
        ],
        scratch_shapes=[pltpu.VMEM((tm, d), BF16)],
        compiler_params=_params(2),
        name="proj",
    )(x2, norm_w, w_main, cos_t, sa_t, sb_t)


def _vt_kernel(wvt_ref, u_ref, o_ref):
    o_ref[0, 0] = _nt(wvt_ref[...], u_ref[...]).astype(BF16)


def _vt_call(wvt, u, batch, tk):
    rows, d = u.shape
    n = wvt.shape[0]
    nblk = rows // batch // tk
    return pl.pallas_call(
        _vt_kernel,
        grid=(batch, nblk),
        in_specs=[
            pl.BlockSpec((n, d), lambda b, s: (0, 0)),
            pl.BlockSpec((tk, d), lambda b, s: (b * nblk + s, 0)),
        ],
        out_specs=pl.BlockSpec((1, 1, n, tk), lambda b, s: (b, s, 0, 0)),
        out_shape=jax.ShapeDtypeStruct((batch, nblk, n, tk), BF16),
        compiler_params=_params(2),
        name="vt",
    )(wvt, u)


def _lg_kernel(u_ref, wlr_ref, w2_ref, b_ref, o_ref):
    g_lr = _nn(u_ref[...], wlr_ref[...])
    gk = _nn(g_lr.astype(BF16), w2_ref[...]) + b_ref[...]
    log_sig = jnp.minimum(gk, 0.0) - jnp.log1p(jnp.exp(-jnp.abs(gk)))
    o_ref[...] = log_sig * (1.0 / GLA_GATE_NORMALIZER)


def _lg_call(u, wlr, w2, bias, tm):
    rows, d = u.shape
    n = w2.shape[1]
    return pl.pallas_call(
        _lg_kernel,
        grid=(rows // tm,),
        in_specs=[
            pl.BlockSpec((tm, d), lambda i: (i, 0)),
            pl.BlockSpec(wlr.shape, lambda i: (0, 0)),
            pl.BlockSpec(w2.shape, lambda i: (0, 0)),
            pl.BlockSpec((1, n), lambda i: (0, 0)),
        ],
        out_specs=pl.BlockSpec((tm, n), lambda i: (i, 0)),
        out_shape=jax.ShapeDtypeStruct((rows, n), F32),
        compiler_params=_params(1),
        name="lg",
    )(u, wlr, w2, bias)


def _attn_kernel(q_ref, k_ref, vt_ref, km_ref, vtm_ref, z_ref, lq1_ref, lk1_ref, lq2_ref,
                 lk2_ref, sw_ref, o_ref, qq_scr, m_scr, l_scr, acc_scr, *, tq, lam_init):
    qi = pl.program_id(2)

    qb = q_ref[...]
    lane = lax.broadcasted_iota(jnp.int32, qb.shape, 1)
    zero = jnp.zeros_like(qb)
    qq_scr[0:tq, :] = jnp.where(lane < DA_HEAD_DIM, qb, zero)
    qq_scr[tq:2 * tq, :] = jnp.where(lane >= DA_HEAD_DIM, qb, zero)

    s = _nt(km_ref[...], qq_scr[...])
    m0 = jnp.max(s, axis=0, keepdims=True)
    p = jnp.exp(s - m0)
    m_scr[...] = m0
    l_scr[...] = jnp.sum(p, axis=0, keepdims=True)
    acc_scr[...] = _nn(vtm_ref[...], p.astype(BF16))

    def step(j, masked):
        kb = k_ref[pl.ds(pl.multiple_of(j * tq, tq), tq), :]
        s = _nt(kb, qq_scr[...])
        if masked:
            key = lax.broadcasted_iota(jnp.int32, s.shape, 0)
            qry = lax.broadcasted_iota(jnp.int32, s.shape, 1)
            qry = jnp.where(qry >= tq, qry - tq, qry)
            s = jnp.where(key <= qry, s, NEG_BIG)
        m_old = m_scr[...]
        m_new = jnp.maximum(m_old, jnp.max(s, axis=0, keepdims=True))
        alpha = jnp.exp(m_old - m_new)
        p = jnp.exp(s - m_new)
        m_scr[...] = m_new
        l_scr[...] = alpha * l_scr[...] + jnp.sum(p, axis=0, keepdims=True)
        acc_scr[...] = alpha * acc_scr[...] + _nn(vt_ref[0, j], p.astype(BF16))

    def body(j, carry):
        step(j, False)
        return carry

    lax.fori_loop(0, qi, body, 0)
    step(qi, True)

    lam = (jnp.exp(jnp.sum(lq1_ref[...] * lk1_ref[...], axis=-1, keepdims=True))
           - jnp.exp(jnp.sum(lq2_ref[...] * lk2_ref[...], axis=-1, keepdims=True))
           + lam_init)
    inv_l = 1.0 / l_scr[...]
    acc = acc_scr[...]
    o = acc[:, 0:tq] * inv_l[:, 0:tq] - lam * (acc[:, tq:2 * tq] * inv_l[:, tq:2 * tq])
    ms = jnp.mean(o * o, axis=0, keepdims=True)
    y = o * lax.rsqrt(ms + NORM_EPS) * sw_ref[...] * (1.0 - lam_init)
    o_ref[...] = (y.T * z_ref[...].astype(F32)).astype(BF16)


def _attn_call(p, vt, pm, vtm, lam_vecs, subln_w, batch, tq, lam_init):
    rows = p.shape[0]
    seq = rows // batch
    nq = seq // tq
    small = pl.BlockSpec((1, DA_HEAD_DIM), lambda b, h, i: (0, 0))
    kernel = functools.partial(_attn_kernel, tq=tq, lam_init=lam_init)
    return pl.pallas_call(
        kernel,
        grid=(batch, DA_HEADS, nq),
        in_specs=[
            pl.BlockSpec((tq, LANES), lambda b, h, i: (b * nq + i, P_Q + h)),
            pl.BlockSpec((seq, LANES), lambda b, h, i: (b, P_K + h)),
            pl.BlockSpec((1, nq, DA_V_DIM, tq), lambda b, h, i: (b, 0, h, 0)),
            pl.BlockSpec((N_META, LANES), lambda b, h, i: (0, P_K + h)),
            pl.BlockSpec((DA_V_DIM, N_META), lambda b, h, i: (h, 0)),
            pl.BlockSpec((tq, LANES), lambda b, h, i: (b * nq + i, P_Z + h)),
            small, small, small, small,
            pl.BlockSpec((DA_V_DIM, 1), lambda b, h, i: (0, 0)),
        ],
        out_specs=pl.BlockSpec((tq, LANES), lambda b, h, i: (b * nq + i, h)),
        out_shape=jax.ShapeDtypeStruct((rows, DA_HEADS * DA_V_DIM), BF16),
        scratch_shapes=[
            pltpu.VMEM((2 * tq, LANES), BF16),
            pltpu.VMEM((1, 2 * tq), F32),
            pltpu.VMEM((1, 2 * tq), F32),
            pltpu.VMEM((DA_V_DIM, 2 * tq), F32),
        ],
        compiler_params=_params(3),
        name="attn",
    )(p, p, vt, pm, vtm, p, *lam_vecs, subln_w)


_GLA_LEVELS = (32, 16, 8, 4, 2, 1)


def _gla_constants():
    c = GLA_CHUNK
    t = np.arange(c)[:, None]
    i = np.arange(c)[None, :]
    mats, masks = [], []
    for s in _GLA_LEVELS:
        upper = (t % (2 * s)) >= s
        r = (t // (2 * s)) * (2 * s) + s - 1
        mats.append(np.where(upper, (i > r) & (i <= t), (i > t) & (i <= r)))
        masks.append(upper & ~upper.T & ((t // (2 * s)) == (i // (2 * s))))
    mats.append(i <= t)
    mats.append(i > t)
    masks.append(t == i)
    m_all = np.concatenate(mats, axis=0).astype(np.float32)
    mask_all = np.stack(masks, axis=0).astype(np.float32)
    tm = np.arange(N_META)
    m_meta = (tm[None, :] > tm[:, None]).astype(np.float32)
    return m_all, mask_all, m_meta


def _split3(x):
    hi = x.astype(BF16)
    r1 = x - hi.astype(F32)
    mid = r1.astype(BF16)
    lo = (r1 - mid.astype(F32)).astype(BF16)
    return hi, mid, lo


def _gate_exponents(mat, lg):
    hi, mid, lo = _split3(lg)
    return _nn(mat, hi) + _nn(mat, mid) + _nn(mat, lo)


def _gla_kernel(q_ref, k_ref, v_ref, lg_ref, z_ref, km_ref, vm_ref, lgm_ref, mall_ref,
                mask_ref, mmeta_ref, nw_ref, o_ref, st_scr, *, tg):
    sblk = pl.program_id(1)
    c = GLA_CHUNK
    nlev = len(_GLA_LEVELS)

    @pl.when(sblk == 0)
    def _():
        for h in range(GLA_HEADS):
            ksl = slice(h * GLA_DK, (h + 1) * GLA_DK)
            vsl = slice(h * GLA_DV, (h + 1) * GLA_DV)
            e = _gate_exponents(mmeta_ref[...], lgm_ref[:, ksl])
            kt = (km_ref[:, ksl].astype(F32) * jnp.exp(e)).astype(BF16)
            st_scr[h] = _tn(vm_ref[:, vsl], kt)

    def chunk(ci, carry):
        r0 = pl.multiple_of(ci * c, c)
        for h in range(GLA_HEADS):
            ksl = slice(h * GLA_DK, (h + 1) * GLA_DK)
            vsl = slice(h * GLA_DV, (h + 1) * GLA_DV)
            q = q_ref[pl.ds(r0, c), ksl]
            k = k_ref[pl.ds(r0, c), ksl]
            v = v_ref[pl.ds(r0, c), vsl]
            f = jnp.exp(_gate_exponents(mall_ref[...], lg_ref[pl.ds(r0, c), ksl]))
            qf = q.astype(F32)
            kf = k.astype(F32)
            a = mask_ref[nlev] * _nt(q, k)
            for lv in range(nlev):
                fl = f[lv * c:(lv + 1) * c]
                a = a + mask_ref[lv] * _nt((qf * fl).astype(BF16), (kf * fl).astype(BF16))
            fb = f[nlev * c:(nlev + 1) * c]
            fk = f[(nlev + 1) * c:(nlev + 2) * c]
            st = st_scr[h]
            o = _nn(a.astype(BF16), v) + _nt((qf * fb).astype(BF16), st.astype(BF16))
            st_scr[h] = st * fb[c - 1:c, :] + _tn(v, (kf * fk).astype(BF16))
            ms = jnp.mean(o * o, axis=-1, keepdims=True)
            y = o * lax.rsqrt(ms + NORM_EPS) * nw_ref[...]
            o_ref[pl.ds(r0, c), vsl] = (y * z_ref[pl.ds(r0, c), vsl].astype(F32)).astype(BF16)
        return carry

    lax.fori_loop(0, tg // c, chunk, 0)


def _gla_call(p, lg, pm, lgm, norm_w, batch, tg):
    rows = p.shape[0]
    nblk = rows // batch // tg
    m_all, mask_all, m_meta = _gla_constants()
    kw, vw = GLA_HEADS * GLA_DK, GLA_HEADS * GLA_DV
    row = lambda b, s: b * nblk + s
    const2 = lambda b, s: (0, 0)
    kernel = functools.partial(_gla_kernel, tg=tg)
    return pl.pallas_call(
        kernel,
        grid=(batch, nblk),
        in_specs=[
            pl.BlockSpec((tg, kw), lambda b, s: (row(b, s), P_GQ * LANES // kw)),
            pl.BlockSpec((tg, kw), lambda b, s: (row(b, s), P_GK * LANES // kw)),
            pl.BlockSpec((tg, vw), lambda b, s: (row(b, s), P_GV * LANES // vw)),
            pl.BlockSpec((tg, kw), lambda b, s: (row(b, s), 0)),
            pl.BlockSpec((tg, vw), lambda b, s: (row(b, s), P_GZ * LANES // vw)),
            pl.BlockSpec((N_META, kw), lambda b, s: (0, P_GK * LANES // kw)),
            pl.BlockSpec((N_META, vw), lambda b, s: (0, P_GV * LANES // vw)),
            pl.BlockSpec((N_META, kw), const2),
            pl.BlockSpec(m_all.shape, const2),
            pl.BlockSpec(mask_all.shape, lambda b, s: (0, 0, 0)),
            pl.BlockSpec(m_meta.shape, const2),
            pl.BlockSpec((1, GLA_DV), const2),
        ],
        out_specs=pl.BlockSpec((tg, vw), lambda b, s: (row(b, s), 0)),
        out_shape=jax.ShapeDtypeStruct((rows, vw), BF16),
        scratch_shapes=[pltpu.VMEM((GLA_HEADS, GLA_DV, GLA_DK), F32)],
        compiler_params=_params(2),
        name="gla",
    )(p, p, p, lg, p, pm, pm, lgm, jnp.asarray(m_all, BF16), jnp.asarray(mask_all, F32),
      jnp.asarray(m_meta, BF16), norm_w)


def _out_kernel(oa_ref, ob_ref, ga_ref, gb_ref, x_ref, wa_ref, wb_ref, wo_ref, fw_ref, o_ref):
    ya = _nn(oa_ref[...], wa_ref[...])
    yb = _nn(ob_ref[...], wb_ref[...])
    merged = ga_ref[...].astype(F32) * ya + gb_ref[...].astype(F32) * yb
    hid = x_ref[...] + _nn(merged.astype(BF16), wo_ref[...])
    ms = jnp.mean(hid * hid, axis=-1, keepdims=True)
    o_ref[...] = hid * lax.rsqrt(ms + NORM_EPS) * fw_ref[...]


def _out_call(oa, ob, p, x2, wa, wb, wo, fw, tm):
    rows, d = x2.shape
    rowblk = lambda i: (i, 0)
    const = lambda i: (0, 0)
    return pl.pallas_call(
        _out_kernel,
        grid=(rows // tm,),
        in_specs=[
            pl.BlockSpec((tm, d), rowblk),
            pl.BlockSpec((tm, d), rowblk),
            pl.BlockSpec((tm, d), lambda i: (i, P_GA * LANES // d)),
            pl.BlockSpec((tm, d), lambda i: (i, P_GB * LANES // d)),
            pl.BlockSpec((tm, d), rowblk),
            pl.BlockSpec((d, d), const),
            pl.BlockSpec((d, d), const),
            pl.BlockSpec((d, d), const),
            pl.BlockSpec((1, d), const),
        ],
        out_specs=pl.BlockSpec((tm, d), rowblk),
        out_shape=jax.ShapeDtypeStruct((rows, d), F32),
        compiler_params=_params(1),
        name="out",
    )(oa, ob, p, p, x2, wa, wb, wo, fw)


def _rope_tables(n_pos):
    half = DA_ROT_DIM // 2
    inv_freq = ROPE_THETA ** (-jnp.arange(half, dtype=F32) / half)
    ang = jnp.arange(n_pos, dtype=jnp.int32).astype(F32)[:, None] * inv_freq[None, :]
    cos, sin = jnp.cos(ang), jnp.sin(ang)
    pad = jnp.zeros((n_pos, DA_HEAD_DIM - DA_ROT_DIM), F32)
    zeros = jnp.zeros((n_pos, half), F32)
    c64 = jnp.concatenate([cos, cos, pad + 1.0], axis=1)
    a64 = jnp.concatenate([-sin, zeros, pad], axis=1)
    b64 = jnp.concatenate([zeros, sin, pad], axis=1)
    two = lambda t: jnp.concatenate([t, t], axis=1)
    return two(c64), two(a64), two(b64)


def kernel(x, meta_tokens, norm_w, w_in, lam_q1, lam_k1, lam_q2, lam_k2, da_subln_w, gla_gate_w2,
           gla_gate_b, gla_norm_w, w_branch_a, w_branch_b, w_out, final_norm_w):
    batch, seq, d = x.shape
    depth = norm_w.shape[0]
    assert depth == 1
    layer = 0
    lam_init = 0.8 - 0.6 * math.exp(-0.3 * layer)
    rows = batch * seq
    x2 = x.reshape(rows, d)

    w = w_in[layer]
    c = np.cumsum([0, 1024, 1024, 1024, 1024, 512, 512, 1024, 1024, GLA_GATE_RANK, 1024, 1024])
    sl = lambda a: w[:, c[a]:c[a + 1]]
    w_main = jnp.concatenate([sl(0), sl(1), sl(3), sl(4), sl(5), sl(6), sl(7), sl(9), sl(10)],
                             axis=1).astype(BF16)
    wvt = sl(2).T.astype(BF16)
    wlr = jnp.pad(sl(8), ((0, 0), (0, LANES - GLA_GATE_RANK))).astype(BF16)
    w2 = jnp.pad(gla_gate_w2[layer], ((0, LANES - GLA_GATE_RANK), (0, 0))).astype(BF16)
    gate_b = gla_gate_b[layer][None, :]
    nw = norm_w[layer][None, :]

    cos_t, sa_t, sb_t = _rope_tables(N_META + seq)
    tabs_meta = tuple(t[:N_META] for t in (cos_t, sa_t, sb_t))
    tabs_real = tuple(t[N_META:] for t in (cos_t, sa_t, sb_t))

    tm_proj = min(1024, seq)
    tq = min(256, seq)
    tg = min(512, seq)
    tm_out = min(512, seq)

    pm, um = _proj_call(meta_tokens.astype(F32), nw, w_main, *tabs_meta, tm=N_META)
    vtm = _vt_call(wvt, um, 1, N_META)[0, 0]
    lgm = _lg_call(um, wlr, w2, gate_b, N_META)

    p, u = _proj_call(x2, nw, w_main, *tabs_real, tm=tm_proj)
    vt = _vt_call(wvt, u, batch, tq)
    lg = _lg_call(u, wlr, w2, gate_b, tm_proj)

    lam_vecs = [v[layer][None, :] for v in (lam_q1, lam_k1, lam_q2, lam_k2)]
    oa = _attn_call(p, vt, pm, vtm, lam_vecs, da_subln_w[layer][:, None], batch, tq, lam_init)
    ob = _gla_call(p, lg, pm, lgm, gla_norm_w[layer][None, :], batch, tg)

    out = _out_call(oa, ob, p, x2, w_branch_a[layer].astype(BF16), w_branch_b[layer].astype(BF16),
                    w_out[layer].astype(BF16), final_norm_w[None, :], tm_out)
    return out.reshape(batch, seq, d)
```

```python
import functools
import math

import numpy as np
import jax
import jax.numpy as jnp
from jax import lax
from jax.experimental import pallas as pl
from jax.experimental.pallas import tpu as pltpu

F32 = jnp.float32
BF16 = jnp.bfloat16

N_META = 16
ROPE_THETA = 500000.0
NORM_EPS = 1e-5

DA_HEADS = 8
DA_HEAD_DIM = 64
DA_V_DIM = 128
DA_ROT_DIM = 16
GLA_HEADS = 4
GLA_DK = 128
GLA_DV = 256
GLA_GATE_RANK = 16
GLA_GATE_NORMALIZER = 16.0
GLA_CHUNK = 64

LANES = 128
VMEM_LIMIT = 48 * 1024 * 1024

P_Q, P_K, P_Z, P_GQ, P_GK, P_GV, P_GZ, P_GA, P_GB = 0, 8, 16, 24, 28, 32, 40, 48, 56
P_COLS = 64 * LANES
PROJ_TN = 512

NEG_BIG = -1e30


def _nt(a, b):
    return lax.dot_general(a, b, (((1,), (1,)), ((), ())), preferred_element_type=F32)


def _tn(a, b):
    return lax.dot_general(a, b, (((0,), (0,)), ((), ())), preferred_element_type=F32)


def _nn(a, b):
    return jnp.dot(a, b, preferred_element_type=F32)


def _params(n_axes):
    return pltpu.CompilerParams(dimension_semantics=("arbitrary",) * n_axes,
                                vmem_limit_bytes=VMEM_LIMIT)


def _proj_kernel(x_ref, nw_ref, w_ref, cos_ref, sa_ref, sb_ref, p_ref, u_ref, u_scr):
    j = pl.program_id(1)

    @pl.when(j == 0)
    def _():
        x = x_ref[...]
        ms = jnp.mean(x * x, axis=-1, keepdims=True)
        u = (x * lax.rsqrt(ms + NORM_EPS) * nw_ref[...]).astype(BF16)
        u_scr[...] = u
        u_ref[...] = u

    def acc():
        return _nn(u_scr[...], w_ref[...])

    def rope(y, scale):
        outs = []
        for g in range(PROJ_TN // LANES):
            t = y[:, g * LANES:(g + 1) * LANES]
            r = (t * cos_ref[...] + pltpu.roll(t, LANES - DA_ROT_DIM // 2, 1) * sa_ref[...]
                 + pltpu.roll(t, DA_ROT_DIM // 2, 1) * sb_ref[...])
            outs.append(r * scale if scale != 1.0 else r)
        return jnp.concatenate(outs, axis=1)

    nq = (P_K - P_Q) * LANES // PROJ_TN
    nk = (P_Z - P_K) * LANES // PROJ_TN
    b_k, b_z, b_gq, b_gk = nq, nq + nk, P_GQ * LANES // PROJ_TN, P_GK * LANES // PROJ_TN
    b_gz, b_ga = P_GZ * LANES // PROJ_TN, P_GA * LANES // PROJ_TN

    @pl.when(j < b_k)
    def _():
        p_ref[...] = rope(acc(), DA_HEAD_DIM ** -0.5 * math.log2(math.e)).astype(BF16)

    @pl.when((j >= b_k) & (j < b_z))
    def _():
        p_ref[...] = rope(acc(), 1.0).astype(BF16)

    @pl.when(((j >= b_z) & (j < b_gq)) | ((j >= b_gz) & (j < b_ga)))
    def _():
        y = acc()
        p_ref[...] = (y * jax.nn.sigmoid(y)).astype(BF16)

    @pl.when(j == b_gq)
    def _():
        p_ref[...] = (acc() * (GLA_DK ** -0.5)).astype(BF16)

    @pl.when((j >= b_gk) & (j < b_gz))
    def _():
        p_ref[...] = acc().astype(BF16)

    @pl.when(j >= b_ga)
    def _():
        p_ref[...] = jax.nn.sigmoid(acc()).astype(BF16)


def _proj_call(x2, norm_w, w_main, cos_t, sa_t, sb_t, tm):
    rows, d = x2.shape
    n_pos_blocks = cos_t.shape[0] // tm
    grid = (rows // tm, P_COLS // PROJ_TN)
    tab_spec = pl.BlockSpec((tm, LANES), lambda i, j: (i % n_pos_blocks, 0))
    return pl.pallas_call(
        _proj_kernel,
        grid=grid,
        in_specs=[
            pl.BlockSpec((tm, d), lambda i, j: (i, 0)),
            pl.BlockSpec((1, d), lambda i, j: (0, 0)),
            pl.BlockSpec((d, PROJ_TN), lambda i, j: (0, j)),
            tab_spec, tab_spec, tab_spec,
        ],
        out_specs=[
            pl.BlockSpec((tm, PROJ_TN), lambda i, j: (i, j)),
            pl.BlockSpec((tm, d), lambda i, j: (i, 0)),
        ],
        out_shape=[
            jax.ShapeDtypeStruct((rows, P_COLS), BF16),
            jax.ShapeDtypeStruct((rows, d), BF16),
        ],
        scratch_shapes=[pltpu.VMEM((tm, d), BF16)],
        compiler_params=_params(2),
        name="proj",
    )(x2, norm_w, w_main, cos_t, sa_t, sb_t)


def _vt_kernel(wvt_ref, u_ref, o_ref):
    o_ref[0, 0] = _nt(wvt_ref[...], u_ref[...]).astype(BF16)


def _vt_call(wvt, u, batch, tk):
    rows, d = u.shape
    n = wvt.shape[0]
    nblk = rows // batch // tk
    return pl.pallas_call(
        _vt_kernel,
        grid=(batch, nblk),
        in_specs=[
            pl.BlockSpec((n, d), lambda b, s: (0, 0)),
            pl.BlockSpec((tk, d), lambda b, s: (b * nblk + s, 0)),
        ],
        out_specs=pl.BlockSpec((1, 1, n, tk), lambda b, s: (b, s, 0, 0)),
        out_shape=jax.ShapeDtypeStruct((batch, nblk, n, tk), BF16),
        compiler_params=_params(2),
        name="vt",
    )(wvt, u)


def _lg_kernel(u_ref, wlr_ref, w2_ref, b_ref, o_ref):
    g_lr = _nn(u_ref[...], wlr_ref[...])
    gk = _nn(g_lr.astype(BF16), w2_ref[...]) + b_ref[...]
    log_sig = jnp.minimum(gk, 0.0) - jnp.log1p(jnp.exp(-jnp.abs(gk)))
    o_ref[...] = log_sig * (1.0 / GLA_GATE_NORMALIZER)


def _lg_call(u, wlr, w2, bias, tm):
    rows, d = u.shape
    n = w2.shape[1]
    return pl.pallas_call(
        _lg_kernel,
        grid=(rows // tm,),
        in_specs=[
            pl.BlockSpec((tm, d), lambda i: (i, 0)),
            pl.BlockSpec(wlr.shape, lambda i: (0, 0)),
            pl.BlockSpec(w2.shape, lambda i: (0, 0)),
            pl.BlockSpec((1, n), lambda i: (0, 0)),
        ],
        out_specs=pl.BlockSpec((tm, n), lambda i: (i, 0)),
        out_shape=jax.ShapeDtypeStruct((rows, n), F32),
        compiler_params=_params(1),
        name="lg",
    )(u, wlr, w2, bias)


ATTN_GROUP = 2


def _attn_kernel(q_ref, k_ref, vt_ref, km_ref, vtm_ref, z_ref, lq1_ref, lk1_ref, lq2_ref,
                 lk2_ref, sw_ref, o_ref, qq_scr, s_scr, sm_scr, mp_scr, lp_scr, acc_scr,
                 *, tq, lam_init):
    qi = pl.program_id(2)
    w = 2 * tq
    grp = ATTN_GROUP

    qb = q_ref[...]
    lane = lax.broadcasted_iota(jnp.int32, qb.shape, 1)
    zero = jnp.zeros_like(qb)
    qq_scr[0:tq, :] = jnp.where(lane < DA_HEAD_DIM, qb, zero)
    qq_scr[tq:w, :] = jnp.where(lane >= DA_HEAD_DIM, qb, zero)

    def fold(x, op):
        return op(x.reshape(x.shape[0] // 8, 8, w), axis=0)

    s_meta = _nt(km_ref[...], qq_scr[...])
    sm_scr[...] = s_meta
    mp_scr[...] = fold(s_meta, jnp.max)

    def scores(j, masked):
        kb = k_ref[pl.ds(pl.multiple_of(j * tq, tq), tq), :]
        s = _nt(kb, qq_scr[...])
        if masked:
            key = lax.broadcasted_iota(jnp.int32, s.shape, 0)
            qry = lax.broadcasted_iota(jnp.int32, s.shape, 1)
            qry = jnp.where(qry >= tq, qry - tq, qry)
            s = jnp.where(key <= qry, s, NEG_BIG)
        s_scr[j] = s
        return fold(s, jnp.max)

    def body1(g, carry):
        mp = scores(g * grp, False)
        for t in range(1, grp):
            mp = jnp.maximum(mp, scores(g * grp + t, False))
        mp_scr[...] = jnp.maximum(mp_scr[...], mp)
        return carry

    n_grp = qi // grp
    rem = qi - n_grp * grp
    lax.fori_loop(0, n_grp, body1, 0)
    for t in range(grp - 1):
        @pl.when(rem > t)
        def _():
            mp_scr[...] = jnp.maximum(mp_scr[...], scores(n_grp * grp + t, False))
    mp_scr[...] = jnp.maximum(mp_scr[...], scores(qi, True))

    m = jnp.max(mp_scr[...], axis=0, keepdims=True)
    p_meta = jnp.exp2(sm_scr[...] - m)
    lp_scr[...] = fold(p_meta, jnp.sum)
    acc_scr[...] = _nn(vtm_ref[...], p_meta.astype(BF16))

    def apply(j):
        p = jnp.exp2(s_scr[j] - m)
        return fold(p, jnp.sum), _nn(vt_ref[0, j], p.astype(BF16))

    def body2(g, carry):
        lp, upd = apply(g * grp)
        for t in range(1, grp):
            lp_t, upd_t = apply(g * grp + t)
            lp, upd = lp + lp_t, upd + upd_t
        lp_scr[...] += lp
        acc_scr[...] += upd
        return carry

    def single(j):
        lp, upd = apply(j)
        lp_scr[...] += lp
        acc_scr[...] += upd

    lax.fori_loop(0, n_grp, body2, 0)
    for t in range(grp - 1):
        @pl.when(rem > t)
        def _():
            single(n_grp * grp + t)
    single(qi)

    lam = (jnp.exp(jnp.sum(lq1_ref[...] * lk1_ref[...], axis=-1, keepdims=True))
           - jnp.exp(jnp.sum(lq2_ref[...] * lk2_ref[...], axis=-1, keepdims=True))
           + lam_init)
    inv_l = 1.0 / jnp.sum(lp_scr[...], axis=0, keepdims=True)
    acc = acc_scr[...]
    o = acc[:, 0:tq] * inv_l[:, 0:tq] - lam * (acc[:, tq:w] * inv_l[:, tq:w])
    ms = jnp.mean(o * o, axis=0, keepdims=True)
    y = o * lax.rsqrt(ms + NORM_EPS) * sw_ref[...] * (1.0 - lam_init)
    o_ref[...] = (y.T * z_ref[...].astype(F32)).astype(BF16)


def _attn_call(p, vt, pm, vtm, lam_vecs, subln_w, batch, tq, lam_init):
    rows = p.shape[0]
    seq = rows // batch
    nq = seq // tq
    small = pl.BlockSpec((1, DA_HEAD_DIM), lambda b, h, i: (0, 0))
    kernel = functools.partial(_attn_kernel, tq=tq, lam_init=lam_init)
    return pl.pallas_call(
        kernel,
        grid=(batch, DA_HEADS, nq),
        in_specs=[
            pl.BlockSpec((tq, LANES), lambda b, h, i: (b * nq + i, P_Q + h)),
            pl.BlockSpec((seq, LANES), lambda b, h, i: (b, P_K + h)),
            pl.BlockSpec((1, nq, DA_V_DIM, tq), lambda b, h, i: (b, 0, h, 0)),
            pl.BlockSpec((N_META, LANES), lambda b, h, i: (0, P_K + h)),
            pl.BlockSpec((DA_V_DIM, N_META), lambda b, h, i: (h, 0)),
            pl.BlockSpec((tq, LANES), lambda b, h, i: (b * nq + i, P_Z + h)),
            small, small, small, small,
            pl.BlockSpec((DA_V_DIM, 1), lambda b, h, i: (0, 0)),
        ],
        out_specs=pl.BlockSpec((tq, LANES), lambda b, h, i: (b * nq + i, h)),
        out_shape=jax.ShapeDtypeStruct((rows, DA_HEADS * DA_V_DIM), BF16),
        scratch_shapes=[
            pltpu.VMEM((2 * tq, LANES), BF16),
            pltpu.VMEM((nq, tq, 2 * tq), F32),
            pltpu.VMEM((N_META, 2 * tq), F32),
            pltpu.VMEM((8, 2 * tq), F32),
            pltpu.VMEM((8, 2 * tq), F32),
            pltpu.VMEM((DA_V_DIM, 2 * tq), F32),
        ],
        compiler_params=_params(3),
        name="attn",
    )(p, p, vt, pm, vtm, p, *lam_vecs, subln_w)


_GLA_LEVELS = (32, 16, 8, 4, 2, 1)


def _gla_constants():
    c = GLA_CHUNK
    t = np.arange(c)[:, None]
    i = np.arange(c)[None, :]
    mats, masks = [], []
    for s in _GLA_LEVELS:
        upper = (t % (2 * s)) >= s
        r = (t // (2 * s)) * (2 * s) + s - 1
        mats.append(np.where(upper, (i > r) & (i <= t), (i > t) & (i <= r)))
        masks.append(upper & ~upper.T & ((t // (2 * s)) == (i // (2 * s))))
    mats.append(i <= t)
    mats.append(i > t)
    masks.append(t == i)
    m_all = np.concatenate(mats, axis=0).astype(np.float32)
    mask_all = np.stack(masks, axis=0).astype(np.float32)
    tm = np.arange(N_META)
    m_meta = (tm[None, :] > tm[:, None]).astype(np.float32)
    return m_all, mask_all, m_meta


def _split3(x):
    hi = x.astype(BF16)
    r1 = x - hi.astype(F32)
    mid = r1.astype(BF16)
    lo = (r1 - mid.astype(F32)).astype(BF16)
    return hi, mid, lo


def _gate_exponents(mat, lg):
    hi, mid, lo = _split3(lg)
    return _nn(mat, hi) + _nn(mat, mid) + _nn(mat, lo)


def _gla_kernel(q_ref, k_ref, v_ref, lg_ref, z_ref, km_ref, vm_ref, lgm_ref, mall_ref,
                mask_ref, mmeta_ref, nw_ref, o_ref, st_scr, *, tg):
    sblk = pl.program_id(1)
    c = GLA_CHUNK
    nlev = len(_GLA_LEVELS)

    @pl.when(sblk == 0)
    def _():
        for h in range(GLA_HEADS):
            ksl = slice(h * GLA_DK, (h + 1) * GLA_DK)
            vsl = slice(h * GLA_DV, (h + 1) * GLA_DV)
            e = _gate_exponents(mmeta_ref[...], lgm_ref[:, ksl])
            kt = (km_ref[:, ksl].astype(F32) * jnp.exp(e)).astype(BF16)
            st_scr[h] = _tn(vm_ref[:, vsl], kt)

    def chunk(ci, carry):
        r0 = pl.multiple_of(ci * c, c)
        for h in range(GLA_HEADS):
            ksl = slice(h * GLA_DK, (h + 1) * GLA_DK)
            vsl = slice(h * GLA_DV, (h + 1) * GLA_DV)
            q = q_ref[pl.ds(r0, c), ksl]
            k = k_ref[pl.ds(r0, c), ksl]
            v = v_ref[pl.ds(r0, c), vsl]
            f = jnp.exp(_gate_exponents(mall_ref[...], lg_ref[pl.ds(r0, c), ksl]))
            qf = q.astype(F32)
            kf = k.astype(F32)
            a = mask_ref[nlev] * _nt(q, k)
            for lv in range(nlev):
                fl = f[lv * c:(lv + 1) * c]
                a = a + mask_ref[lv] * _nt((qf * fl).astype(BF16), (kf * fl).astype(BF16))
            fb = f[nlev * c:(nlev + 1) * c]
            fk = f[(nlev + 1) * c:(nlev + 2) * c]
            st = st_scr[h]
            o = _nn(a.astype(BF16), v) + _nt((qf * fb).astype(BF16), st.astype(BF16))
            st_scr[h] = st * fb[c - 1:c, :] + _tn(v, (kf * fk).astype(BF16))
            ms = jnp.mean(o * o, axis=-1, keepdims=True)
            y = o * lax.rsqrt(ms + NORM_EPS) * nw_ref[...]
            o_ref[pl.ds(r0, c), vsl] = (y * z_ref[pl.ds(r0, c), vsl].astype(F32)).astype(BF16)
        return carry

    lax.fori_loop(0, tg // c, chunk, 0)


def _gla_call(p, lg, pm, lgm, norm_w, batch, tg):
    rows = p.shape[0]
    nblk = rows // batch // tg
    m_all, mask_all, m_meta = _gla_constants()
    kw, vw = GLA_HEADS * GLA_DK, GLA_HEADS * GLA_DV
    row = lambda b, s: b * nblk + s
    const2 = lambda b, s: (0, 0)
    kernel = functools.partial(_gla_kernel, tg=tg)
    return pl.pallas_call(
        kernel,
        grid=(batch, nblk),
        in_specs=[
            pl.BlockSpec((tg, kw), lambda b, s: (row(b, s), P_GQ * LANES // kw)),
            pl.BlockSpec((tg, kw), lambda b, s: (row(b, s), P_GK * LANES // kw)),
            pl.BlockSpec((tg, vw), lambda b, s: (row(b, s), P_GV * LANES // vw)),
            pl.BlockSpec((tg, kw), lambda b, s: (row(b, s), 0)),
            pl.BlockSpec((tg, vw), lambda b, s: (row(b, s), P_GZ * LANES // vw)),
            pl.BlockSpec((N_META, kw), lambda b, s: (0, P_GK * LANES // kw)),
            pl.BlockSpec((N_META, vw), lambda b, s: (0, P_GV * LANES // vw)),
            pl.BlockSpec((N_META, kw), const2),
            pl.BlockSpec(m_all.shape, const2),
            pl.BlockSpec(mask_all.shape, lambda b, s: (0, 0, 0)),
            pl.BlockSpec(m_meta.shape, const2),
            pl.BlockSpec((1, GLA_DV), const2),
        ],
        out_specs=pl.BlockSpec((tg, vw), lambda b, s: (row(b, s), 0)),
        out_shape=jax.ShapeDtypeStruct((rows, vw), BF16),
        scratch_shapes=[pltpu.VMEM((GLA_HEADS, GLA_DV, GLA_DK), F32)],
        compiler_params=_params(2),
        name="gla",
    )(p, p, p, lg, p, pm, pm, lgm, jnp.asarray(m_all, BF16), jnp.asarray(mask_all, F32),
      jnp.asarray(m_meta, BF16), norm_w)


def _out_kernel(oa_ref, ob_ref, ga_ref, gb_ref, x_ref, wa_ref, wb_ref, wo_ref, fw_ref, o_ref):
    ya = _nn(oa_ref[...], wa_ref[...])
    yb = _nn(ob_ref[...], wb_ref[...])
    merged = ga_ref[...].astype(F32) * ya + gb_ref[...].astype(F32) * yb
    hid = x_ref[...] + _nn(merged.astype(BF16), wo_ref[...])
    ms = jnp.mean(hid * hid, axis=-1, keepdims=True)
    o_ref[...] = hid * lax.rsqrt(ms + NORM_EPS) * fw_ref[...]


def _out_call(oa, ob, p, x2, wa, wb, wo, fw, tm):
    rows, d = x2.shape
    rowblk = lambda i: (i, 0)
    const = lambda i: (0, 0)
    return pl.pallas_call(
        _out_kernel,
        grid=(rows // tm,),
        in_specs=[
            pl.BlockSpec((tm, d), rowblk),
            pl.BlockSpec((tm, d), rowblk),
            pl.BlockSpec((tm, d), lambda i: (i, P_GA * LANES // d)),
            pl.BlockSpec((tm, d), lambda i: (i, P_GB * LANES // d)),
            pl.BlockSpec((tm, d), rowblk),
            pl.BlockSpec((d, d), const),
            pl.BlockSpec((d, d), const),
            pl.BlockSpec((d, d), const),
            pl.BlockSpec((1, d), const),
        ],
        out_specs=pl.BlockSpec((tm, d), rowblk),
        out_shape=jax.ShapeDtypeStruct((rows, d), F32),
        compiler_params=_params(1),
        name="out",
    )(oa, ob, p, p, x2, wa, wb, wo, fw)


def _rope_tables(n_pos):
    half = DA_ROT_DIM // 2
    inv_freq = ROPE_THETA ** (-jnp.arange(half, dtype=F32) / half)
    ang = jnp.arange(n_pos, dtype=jnp.int32).astype(F32)[:, None] * inv_freq[None, :]
    cos, sin = jnp.cos(ang), jnp.sin(ang)
    pad = jnp.zeros((n_pos, DA_HEAD_DIM - DA_ROT_DIM), F32)
    zeros = jnp.zeros((n_pos, half), F32)
    c64 = jnp.concatenate([cos, cos, pad + 1.0], axis=1)
    a64 = jnp.concatenate([-sin, zeros, pad], axis=1)
    b64 = jnp.concatenate([zeros, sin, pad], axis=1)
    two = lambda t: jnp.concatenate([t, t], axis=1)
    return two(c64), two(a64), two(b64)


def kernel(x, meta_tokens, norm_w, w_in, lam_q1, lam_k1, lam_q2, lam_k2, da_subln_w, gla_gate_w2,
           gla_gate_b, gla_norm_w, w_branch_a, w_branch_b, w_out, final_norm_w):
    batch, seq, d = x.shape
    depth = norm_w.shape[0]
    assert depth == 1
    layer = 0
    lam_init = 0.8 - 0.6 * math.exp(-0.3 * layer)
    rows = batch * seq
    x2 = x.reshape(rows, d)

    w = w_in[layer]
    c = np.cumsum([0, 1024, 1024, 1024, 1024, 512, 512, 1024, 1024, GLA_GATE_RANK, 1024, 1024])
    sl = lambda a: w[:, c[a]:c[a + 1]]
    w_main = jnp.concatenate([sl(0), sl(1), sl(3), sl(4), sl(5), sl(6), sl(7), sl(9), sl(10)],
                             axis=1).astype(BF16)
    wvt = sl(2).T.astype(BF16)
    wlr = jnp.pad(sl(8), ((0, 0), (0, LANES - GLA_GATE_RANK))).astype(BF16)
    w2 = jnp.pad(gla_gate_w2[layer], ((0, LANES - GLA_GATE_RANK), (0, 0))).astype(BF16)
    gate_b = gla_gate_b[layer][None, :]
    nw = norm_w[layer][None, :]

    cos_t, sa_t, sb_t = _rope_tables(N_META + seq)
    tabs_meta = tuple(t[:N_META] for t in (cos_t, sa_t, sb_t))
    tabs_real = tuple(t[N_META:] for t in (cos_t, sa_t, sb_t))

    tm_proj = min(1024, seq)
    tq = min(512, seq)
    tg = min(512, seq)
    tm_out = min(512, seq)

    pm, um = _proj_call(meta_tokens.astype(F32), nw, w_main, *tabs_meta, tm=N_META)
    vtm = _vt_call(wvt, um, 1, N_META)[0, 0]
    lgm = _lg_call(um, wlr, w2, gate_b, N_META)

    p, u = _proj_call(x2, nw, w_main, *tabs_real, tm=tm_proj)
    vt = _vt_call(wvt, u, batch, tq)
    lg = _lg_call(u, wlr, w2, gate_b, tm_proj)

    lam_vecs = [v[layer][None, :] for v in (lam_q1, lam_k1, lam_q2, lam_k2)]
    oa = _attn_call(p, vt, pm, vtm, lam_vecs, da_subln_w[layer][:, None], batch, tq, lam_init)
    ob = _gla_call(p, lg, pm, lgm, gla_norm_w[layer][None, :], batch, tg)

    out = _out_call(oa, ob, p, x2, w_branch_a[layer].astype(BF16), w_branch_b[layer].astype(BF16),
                    w_out[layer].astype(BF16), final_norm_w[None, :], tm_out)
    return out.reshape(batch, seq, d)
```

```python
import functools
import math

import numpy as np
import jax
import jax.numpy as jnp
from jax import lax
from jax.experimental import pallas as pl
from jax.experimental.pallas import tpu as pltpu

F32 = jnp.float32
BF16 = jnp.bfloat16

N_META = 16
ROPE_THETA = 500000.0
NORM_EPS = 1e-5

DA_HEADS = 8
DA_HEAD_DIM = 64
DA_V_DIM = 128
DA_ROT_DIM = 16
GLA_HEADS = 4
GLA_DK = 128
GLA_DV = 256
GLA_GATE_RANK = 16
GLA_GATE_NORMALIZER = 16.0
GLA_CHUNK = 128

LANES = 128
VMEM_LIMIT = 48 * 1024 * 1024

P_Q, P_K, P_Z, P_GQ, P_GK, P_GV, P_GZ, P_GA, P_GB = 0, 8, 16, 24, 28, 32, 40, 48, 56
P_COLS = 64 * LANES
PROJ_TN = 512
PROJ_RC = 256

NEG_BIG = -1e30


def _nt(a, b):
    return lax.dot_general(a, b, (((1,), (1,)), ((), ())), preferred_element_type=F32)


def _tn(a, b):
    return lax.dot_general(a, b, (((0,), (0,)), ((), ())), preferred_element_type=F32)


def _nn(a, b):
    return jnp.dot(a, b, preferred_element_type=F32)


def _params(n_axes):
    return pltpu.CompilerParams(dimension_semantics=("arbitrary",) * n_axes,
                                vmem_limit_bytes=VMEM_LIMIT)


def _proj_kernel(x_ref, nw_ref, w_ref, cq_ref, sq_ref, ck_ref, sk_ref, p_ref, u_ref, u_scr,
                 *, rc):
    j = pl.program_id(1)
    tm = x_ref.shape[0]

    @pl.when(j == 0)
    def _():
        x = x_ref[...]
        ms = jnp.mean(x * x, axis=-1, keepdims=True)
        u = (x * lax.rsqrt(ms + NORM_EPS) * nw_ref[...]).astype(BF16)
        u_scr[...] = u
        u_ref[...] = u

    def run(epilogue):
        for r in range(tm // rc):
            rows = slice(r * rc, (r + 1) * rc)
            y = _nn(u_scr[rows, :], w_ref[...])
            p_ref[rows, :] = epilogue(y, rows).astype(BF16)

    def rope(c_ref, s_ref):
        def epilogue(y, rows):
            c, s = c_ref[rows, :], s_ref[rows, :]
            outs = []
            for g in range(PROJ_TN // LANES):
                t = y[:, g * LANES:(g + 1) * LANES]
                outs.append(t * c + pltpu.roll(t, LANES // 2, 1) * s)
            return jnp.concatenate(outs, axis=1)
        return epilogue

    nq = (P_K - P_Q) * LANES // PROJ_TN
    nk = (P_Z - P_K) * LANES // PROJ_TN
    b_k, b_z, b_gq, b_gk = nq, nq + nk, P_GQ * LANES // PROJ_TN, P_GK * LANES // PROJ_TN
    b_gz, b_ga = P_GZ * LANES // PROJ_TN, P_GA * LANES // PROJ_TN

    @pl.when(j < b_k)
    def _():
        run(rope(cq_ref, sq_ref))

    @pl.when((j >= b_k) & (j < b_z))
    def _():
        run(rope(ck_ref, sk_ref))

    @pl.when(((j >= b_z) & (j < b_gq)) | ((j >= b_gz) & (j < b_ga)))
    def _():
        run(lambda y, rows: y * jax.nn.sigmoid(y))

    @pl.when(j == b_gq)
    def _():
        run(lambda y, rows: y * (GLA_DK ** -0.5))

    @pl.when((j >= b_gk) & (j < b_gz))
    def _():
        run(lambda y, rows: y)

    @pl.when(j >= b_ga)
    def _():
        run(lambda y, rows: jax.nn.sigmoid(y))


def _proj_call(x2, norm_w, w_main, tabs, tm):
    rows, d = x2.shape
    n_pos_blocks = tabs[0].shape[0] // tm
    grid = (rows // tm, P_COLS // PROJ_TN)
    tab_spec = pl.BlockSpec((tm, LANES), lambda i, j: (i % n_pos_blocks, 0))
    kernel = functools.partial(_proj_kernel, rc=min(PROJ_RC, tm))
    return pl.pallas_call(
        kernel,
        grid=grid,
        in_specs=[
            pl.BlockSpec((tm, d), lambda i, j: (i, 0)),
            pl.BlockSpec((1, d), lambda i, j: (0, 0)),
            pl.BlockSpec((d, PROJ_TN), lambda i, j: (0, j)),
            tab_spec, tab_spec, tab_spec, tab_spec,
        ],
        out_specs=[
            pl.BlockSpec((tm, PROJ_TN), lambda i, j: (i, j)),
            pl.BlockSpec((tm, d), lambda i, j: (i, 0)),
        ],
        out_shape=[
            jax.ShapeDtypeStruct((rows, P_COLS), BF16),
            jax.ShapeDtypeStruct((rows, d), BF16),
        ],
        scratch_shapes=[pltpu.VMEM((tm, d), BF16)],
        compiler_params=_params(2),
        name="proj",
    )(x2, norm_w, w_main, *tabs)


def _vt_kernel(wvt_ref, u_ref, o_ref):
    o_ref[0, 0] = _nt(wvt_ref[...], u_ref[...]).astype(BF16)


def _vt_call(wvt, u, batch, tk):
    rows, d = u.shape
    n = wvt.shape[0]
    nblk = rows // batch // tk
    return pl.pallas_call(
        _vt_kernel,
        grid=(batch, nblk),
        in_specs=[
            pl.BlockSpec((n, d), lambda b, s: (0, 0)),
            pl.BlockSpec((tk, d), lambda b, s: (b * nblk + s, 0)),
        ],
        out_specs=pl.BlockSpec((1, 1, n, tk), lambda b, s: (b, s, 0, 0)),
        out_shape=jax.ShapeDtypeStruct((batch, nblk, n, tk), BF16),
        compiler_params=_params(2),
        name="vt",
    )(wvt, u)


def _lg_kernel(u_ref, wlr_ref, w2_ref, b_ref, o_ref):
    g_lr = _nn(u_ref[...], wlr_ref[...])
    gk = _nn(g_lr.astype(BF16), w2_ref[...]) + b_ref[...]
    log_sig = jnp.minimum(gk, 0.0) - jnp.log1p(jnp.exp(-jnp.abs(gk)))
    o_ref[...] = log_sig * (math.log2(math.e) / GLA_GATE_NORMALIZER)


def _lg_call(u, wlr, w2, bias, tm):
    rows, d = u.shape
    n = w2.shape[1]
    return pl.pallas_call(
        _lg_kernel,
        grid=(rows // tm,),
        in_specs=[
            pl.BlockSpec((tm, d), lambda i: (i, 0)),
            pl.BlockSpec(wlr.shape, lambda i: (0, 0)),
            pl.BlockSpec(w2.shape, lambda i: (0, 0)),
            pl.BlockSpec((1, n), lambda i: (0, 0)),
        ],
        out_specs=pl.BlockSpec((tm, n), lambda i: (i, 0)),
        out_shape=jax.ShapeDtypeStruct((rows, n), F32),
        compiler_params=_params(1),
        name="lg",
    )(u, wlr, w2, bias)


ATTN_GROUP = 2


def _attn_kernel(q_ref, k_ref, vt_ref, km_ref, vtm_ref, z_ref, map1_ref, bias_ref, lq1_ref,
                 lk1_ref, lq2_ref, lk2_ref, sw_ref, o_ref, qq_scr, s_scr, sm_scr, mp_scr,
                 lp_scr, acc_scr, *, tq, lam_init):
    qi = pl.program_id(2)
    w = 2 * tq
    grp = ATTN_GROUP

    qb = q_ref[...]
    in_map1 = jnp.broadcast_to(map1_ref[...], qb.shape) > 0.5
    zero = jnp.zeros_like(qb)
    qq_scr[0:tq, :] = jnp.where(in_map1, qb, zero)
    qq_scr[tq:w, :] = jnp.where(in_map1, zero, qb)

    def fold(x, op):
        return op(x.reshape(x.shape[0] // 8, 8, w), axis=0)

    s_meta = _nt(km_ref[...], qq_scr[...])
    sm_scr[...] = s_meta
    mp_scr[...] = fold(s_meta, jnp.max)

    def scores(j, masked):
        kb = k_ref[pl.ds(pl.multiple_of(j * tq, tq), tq), :]
        s = _nt(kb, qq_scr[...])
        if masked:
            s = s + bias_ref[...]
        s_scr[j] = s
        return fold(s, jnp.max)

    def body1(g, carry):
        mp = scores(g * grp, False)
        for t in range(1, grp):
            mp = jnp.maximum(mp, scores(g * grp + t, False))
        mp_scr[...] = jnp.maximum(mp_scr[...], mp)
        return carry

    n_grp = qi // grp
    rem = qi - n_grp * grp
    lax.fori_loop(0, n_grp, body1, 0)
    for t in range(grp - 1):
        @pl.when(rem > t)
        def _():
            mp_scr[...] = jnp.maximum(mp_scr[...], scores(n_grp * grp + t, False))
    mp_scr[...] = jnp.maximum(mp_scr[...], scores(qi, True))

    m = jnp.max(mp_scr[...], axis=0, keepdims=True)
    p_meta = jnp.exp2(sm_scr[...] - m)
    lp_scr[...] = fold(p_meta, jnp.sum)
    acc_scr[...] = _nn(vtm_ref[...], p_meta.astype(BF16))

    def apply(j):
        p = jnp.exp2(s_scr[j] - m)
        return fold(p, jnp.sum), _nn(vt_ref[0, j], p.astype(BF16))

    def body2(g, carry):
        lp, upd = apply(g * grp)
        for t in range(1, grp):
            lp_t, upd_t = apply(g * grp + t)
            lp, upd = lp + lp_t, upd + upd_t
        lp_scr[...] += lp
        acc_scr[...] += upd
        return carry

    def single(j):
        lp, upd = apply(j)
        lp_scr[...] += lp
        acc_scr[...] += upd

    lax.fori_loop(0, n_grp, body2, 0)
    for t in range(grp - 1):
        @pl.when(rem > t)
        def _():
            single(n_grp * grp + t)
    single(qi)

    lam = (jnp.exp(jnp.sum(lq1_ref[...] * lk1_ref[...], axis=-1, keepdims=True))
           - jnp.exp(jnp.sum(lq2_ref[...] * lk2_ref[...], axis=-1, keepdims=True))
           + lam_init)
    inv_l = 1.0 / jnp.sum(lp_scr[...], axis=0, keepdims=True)
    acc = acc_scr[...]
    o = acc[:, 0:tq] * inv_l[:, 0:tq] - lam * (acc[:, tq:w] * inv_l[:, tq:w])
    ms = jnp.mean(o * o, axis=0, keepdims=True)
    y = o * lax.rsqrt(ms + NORM_EPS) * sw_ref[...] * (1.0 - lam_init)
    o_ref[...] = (y.T * z_ref[...].astype(F32)).astype(BF16)


def _attn_call(p, vt, pm, vtm, map1_lanes, lam_vecs, subln_w, batch, tq, lam_init):
    rows = p.shape[0]
    seq = rows // batch
    nq = seq // tq
    small = pl.BlockSpec((1, DA_HEAD_DIM), lambda b, h, i: (0, 0))
    key = np.arange(tq)[:, None]
    qry = np.arange(2 * tq)[None, :] % tq
    bias = jnp.asarray(np.where(key <= qry, 0.0, NEG_BIG), F32)
    kernel = functools.partial(_attn_kernel, tq=tq, lam_init=lam_init)
    return pl.pallas_call(
        kernel,
        grid=(batch, DA_HEADS, nq),
        in_specs=[
            pl.BlockSpec((tq, LANES), lambda b, h, i: (b * nq + i, P_Q + h)),
            pl.BlockSpec((seq, LANES), lambda b, h, i: (b, P_K + h)),
            pl.BlockSpec((1, nq, DA_V_DIM, tq), lambda b, h, i: (b, 0, h, 0)),
            pl.BlockSpec((N_META, LANES), lambda b, h, i: (0, P_K + h)),
            pl.BlockSpec((DA_V_DIM, N_META), lambda b, h, i: (h, 0)),
            pl.BlockSpec((tq, LANES), lambda b, h, i: (b * nq + i, P_Z + h)),
            pl.BlockSpec((1, LANES), lambda b, h, i: (0, 0)),
            pl.BlockSpec((tq, 2 * tq), lambda b, h, i: (0, 0)),
            small, small, small, small,
            pl.BlockSpec((DA_V_DIM, 1), lambda b, h, i: (0, 0)),
        ],
        out_specs=pl.BlockSpec((tq, LANES), lambda b, h, i: (b * nq + i, h)),
        out_shape=jax.ShapeDtypeStruct((rows, DA_HEADS * DA_V_DIM), BF16),
        scratch_shapes=[
            pltpu.VMEM((2 * tq, LANES), BF16),
            pltpu.VMEM((nq, tq, 2 * tq), F32),
            pltpu.VMEM((N_META, 2 * tq), F32),
            pltpu.VMEM((8, 2 * tq), F32),
            pltpu.VMEM((8, 2 * tq), F32),
            pltpu.VMEM((DA_V_DIM, 2 * tq), F32),
        ],
        compiler_params=_params(3),
        name="attn",
    )(p, p, vt, pm, vtm, p, map1_lanes, bias, *lam_vecs, subln_w)


_GLA_LEVELS = tuple(GLA_CHUNK >> (i + 1) for i in range(GLA_CHUNK.bit_length() - 1))


def _gla_constants():
    c = GLA_CHUNK
    t = np.arange(c)[:, None]
    i = np.arange(c)[None, :]
    mats, masks = [], []
    for s in _GLA_LEVELS:
        upper = (t % (2 * s)) >= s
        r = (t // (2 * s)) * (2 * s) + s - 1
        mats.append(np.where(upper, (i > r) & (i <= t), (i > t) & (i <= r)))
        masks.append(upper & ~upper.T & ((t // (2 * s)) == (i // (2 * s))))
    mats.append(i <= t)
    mats.append(i > t)
    masks.append(t == i)
    m_all = np.concatenate(mats, axis=0).astype(np.float32)
    mask_all = np.stack(masks, axis=0).astype(np.float32)
    tm = np.arange(N_META)
    m_meta = (tm[None, :] > tm[:, None]).astype(np.float32)
    twice = lambda m: np.concatenate([m, m], axis=1)
    return twice(m_all), mask_all, twice(m_meta)


def _gate_factors(mat2, lg2):
    hi = lg2.astype(BF16)
    lo = (lg2 - hi.astype(F32)).astype(BF16)
    return jnp.exp2(_nn(mat2, jnp.concatenate([hi, lo], axis=0)))


def _gla_kernel(q_ref, k_ref, v_ref, lg_ref, z_ref, km_ref, vm_ref, lgm_ref, mall_ref,
                mask_ref, mmeta_ref, nw_ref, o_ref, st_scr, *, tg):
    sblk = pl.program_id(1)
    c = GLA_CHUNK
    nlev = len(_GLA_LEVELS)

    @pl.when(sblk == 0)
    def _():
        f_meta = _gate_factors(mmeta_ref[...], lgm_ref[...])
        for h in range(GLA_HEADS):
            ksl = slice(h * GLA_DK, (h + 1) * GLA_DK)
            vsl = slice(h * GLA_DV, (h + 1) * GLA_DV)
            kt = (km_ref[:, ksl].astype(F32) * f_meta[:, ksl]).astype(BF16)
            st_scr[h] = _tn(vm_ref[:, vsl], kt)

    def chunk(ci, carry):
        r0 = pl.multiple_of(ci * c, c)
        f_all = _gate_factors(mall_ref[...], lg_ref[pl.ds(r0, c), :])
        for h in range(GLA_HEADS):
            ksl = slice(h * GLA_DK, (h + 1) * GLA_DK)
            vsl = slice(h * GLA_DV, (h + 1) * GLA_DV)
            q = q_ref[pl.ds(r0, c), ksl]
            k = k_ref[pl.ds(r0, c), ksl]
            v = v_ref[pl.ds(r0, c), vsl]
            qf = q.astype(F32)
            kf = k.astype(F32)
            a = jnp.where(mask_ref[nlev] > 0.5, _nt(q, k), 0.0)
            for lv in range(nlev):
                fl = f_all[lv * c:(lv + 1) * c, ksl]
                a = jnp.where(mask_ref[lv] > 0.5,
                              _nt((qf * fl).astype(BF16), (kf * fl).astype(BF16)), a)
            fb = f_all[nlev * c:(nlev + 1) * c, ksl]
            fk = f_all[(nlev + 1) * c:(nlev + 2) * c, ksl]
            st = st_scr[h]
            o = _nn(a.astype(BF16), v) + _nt((qf * fb).astype(BF16), st.astype(BF16))
            st_scr[h] = st * fb[c - 1:c, :] + _tn(v, (kf * fk).astype(BF16))
            ms = jnp.mean(o * o, axis=-1, keepdims=True)
            y = o * lax.rsqrt(ms + NORM_EPS) * nw_ref[...]
            o_ref[pl.ds(r0, c), vsl] = (y * z_ref[pl.ds(r0, c), vsl].astype(F32)).astype(BF16)
        return carry

    lax.fori_loop(0, tg // c, chunk, 0)


def _gla_call(p, lg, pm, lgm, norm_w, batch, tg):
    rows = p.shape[0]
    nblk = rows // batch // tg
    m_all, mask_all, m_meta = _gla_constants()
    kw, vw = GLA_HEADS * GLA_DK, GLA_HEADS * GLA_DV
    row = lambda b, s: b * nblk + s
    const2 = lambda b, s: (0, 0)
    kernel = functools.partial(_gla_kernel, tg=tg)
    return pl.pallas_call(
        kernel,
        grid=(batch, nblk),
        in_specs=[
            pl.BlockSpec((tg, kw), lambda b, s: (row(b, s), P_GQ * LANES // kw)),
            pl.BlockSpec((tg, kw), lambda b, s: (row(b, s), P_GK * LANES // kw)),
            pl.BlockSpec((tg, vw), lambda b, s: (row(b, s), P_GV * LANES // vw)),
            pl.BlockSpec((tg, kw), lambda b, s: (row(b, s), 0)),
            pl.BlockSpec((tg, vw), lambda b, s: (row(b, s), P_GZ * LANES // vw)),
            pl.BlockSpec((N_META, kw), lambda b, s: (0, P_GK * LANES // kw)),
            pl.BlockSpec((N_META, vw), lambda b, s: (0, P_GV * LANES // vw)),
            pl.BlockSpec((N_META, kw), const2),
            pl.BlockSpec(m_all.shape, const2),
            pl.BlockSpec(mask_all.shape, lambda b, s: (0, 0, 0)),
            pl.BlockSpec(m_meta.shape, const2),
            pl.BlockSpec((1, GLA_DV), const2),
        ],
        out_specs=pl.BlockSpec((tg, vw), lambda b, s: (row(b, s), 0)),
        out_shape=jax.ShapeDtypeStruct((rows, vw), BF16),
        scratch_shapes=[pltpu.VMEM((GLA_HEADS, GLA_DV, GLA_DK), F32)],
        compiler_params=_params(2),
        name="gla",
    )(p, p, p, lg, p, pm, pm, lgm, jnp.asarray(m_all, BF16), jnp.asarray(mask_all, F32),
      jnp.asarray(m_meta, BF16), norm_w)


def _out_kernel(oa_ref, ob_ref, ga_ref, gb_ref, x_ref, wa_ref, wb_ref, wo_ref, fw_ref, o_ref):
    ya = _nn(oa_ref[...], wa_ref[...])
    yb = _nn(ob_ref[...], wb_ref[...])
    merged = ga_ref[...].astype(F32) * ya + gb_ref[...].astype(F32) * yb
    hid = x_ref[...] + _nn(merged.astype(BF16), wo_ref[...])
    ms = jnp.mean(hid * hid, axis=-1, keepdims=True)
    o_ref[...] = hid * lax.rsqrt(ms + NORM_EPS) * fw_ref[...]


def _out_call(oa, ob, p, x2, wa, wb, wo, fw, tm):
    rows, d = x2.shape
    rowblk = lambda i: (i, 0)
    const = lambda i: (0, 0)
    return pl.pallas_call(
        _out_kernel,
        grid=(rows // tm,),
        in_specs=[
            pl.BlockSpec((tm, d), rowblk),
            pl.BlockSpec((tm, d), rowblk),
            pl.BlockSpec((tm, d), lambda i: (i, P_GA * LANES // d)),
            pl.BlockSpec((tm, d), lambda i: (i, P_GB * LANES // d)),
            pl.BlockSpec((tm, d), rowblk),
            pl.BlockSpec((d, d), const),
            pl.BlockSpec((d, d), const),
            pl.BlockSpec((d, d), const),
            pl.BlockSpec((1, d), const),
        ],
        out_specs=pl.BlockSpec((tm, d), rowblk),
        out_shape=jax.ShapeDtypeStruct((rows, d), F32),
        compiler_params=_params(1),
        name="out",
    )(oa, ob, p, p, x2, wa, wb, wo, fw)


def _head_lane_layout():
    half_r = DA_ROT_DIM // 2
    n_plain = (DA_HEAD_DIM - DA_ROT_DIM) // 2
    src, is_map1 = [], []
    for half in range(2):
        rot0 = half * half_r
        plain0 = DA_ROT_DIM + half * n_plain
        for which, d0, n in ((0, rot0, half_r), (1, rot0, half_r),
                             (0, plain0, n_plain), (1, plain0, n_plain)):
            src += [which * DA_HEAD_DIM + d for d in range(d0, d0 + n)]
            is_map1 += [which == 0] * n
    return np.array(src), np.array(is_map1)


def _rope_tables(n_pos, scale):
    half_r = DA_ROT_DIM // 2
    inv_freq = ROPE_THETA ** (-jnp.arange(half_r, dtype=F32) / half_r)
    ang = jnp.arange(n_pos, dtype=jnp.int32).astype(F32)[:, None] * inv_freq[None, :]
    cos, sin = jnp.cos(ang), jnp.sin(ang)
    ones = jnp.ones((n_pos, DA_HEAD_DIM - DA_ROT_DIM), F32)
    c_tab = jnp.concatenate([cos, cos, ones, cos, cos, ones], axis=1) * scale
    s_tab = jnp.concatenate([-sin, -sin, 0.0 * ones, sin, sin, 0.0 * ones], axis=1) * scale
    return c_tab, s_tab


def kernel(x, meta_tokens, norm_w, w_in, lam_q1, lam_k1, lam_q2, lam_k2, da_subln_w, gla_gate_w2,
           gla_gate_b, gla_norm_w, w_branch_a, w_branch_b, w_out, final_norm_w):
    batch, seq, d = x.shape
    depth = norm_w.shape[0]
    assert depth == 1
    layer = 0
    lam_init = 0.8 - 0.6 * math.exp(-0.3 * layer)
    rows = batch * seq
    x2 = x.reshape(rows, d)

    w = w_in[layer]
    c = np.cumsum([0, 1024, 1024, 1024, 1024, 512, 512, 1024, 1024, GLA_GATE_RANK, 1024, 1024])
    sl = lambda a: w[:, c[a]:c[a + 1]]
    lane_src, lane_is_map1 = _head_lane_layout()
    qk_perm = (np.arange(DA_HEADS)[:, None] * (2 * DA_HEAD_DIM) + lane_src[None, :]).reshape(-1)
    w_main = jnp.concatenate([sl(0)[:, qk_perm], sl(1)[:, qk_perm], sl(3), sl(4), sl(5), sl(6),
                              sl(7), sl(9), sl(10)], axis=1).astype(BF16)
    wvt = sl(2).T.astype(BF16)
    wlr = jnp.pad(sl(8), ((0, 0), (0, LANES - GLA_GATE_RANK))).astype(BF16)
    w2 = jnp.pad(gla_gate_w2[layer], ((0, LANES - GLA_GATE_RANK), (0, 0))).astype(BF16)
    gate_b = gla_gate_b[layer][None, :]
    nw = norm_w[layer][None, :]

    q_scale = DA_HEAD_DIM ** -0.5 * math.log2(math.e)
    tabs = _rope_tables(N_META + seq, q_scale) + _rope_tables(N_META + seq, 1.0)
    tabs_meta = tuple(t[:N_META] for t in tabs)
    tabs_real = tuple(t[N_META:] for t in tabs)
    map1_lanes = jnp.asarray(lane_is_map1[None, :], F32)

    tm_proj = min(1024, seq)
    tq = min(512, seq)
    tg = min(512, seq)
    tm_out = min(512, seq)

    pm, um = _proj_call(meta_tokens.astype(F32), nw, w_main, tabs_meta, tm=N_META)
    vtm = _vt_call(wvt, um, 1, N_META)[0, 0]
    lgm = _lg_call(um, wlr, w2, gate_b, N_META)

    p, u = _proj_call(x2, nw, w_main, tabs_real, tm=tm_proj)
    vt = _vt_call(wvt, u, batch, tq)
    lg = _lg_call(u, wlr, w2, gate_b, tm_proj)

    lam_vecs = [v[layer][None, :] for v in (lam_q1, lam_k1, lam_q2, lam_k2)]
    oa = _attn_call(p, vt, pm, vtm, map1_lanes, lam_vecs, da_subln_w[layer][:, None], batch, tq,
                    lam_init)
    ob = _gla_call(p, lg, pm, lgm, gla_norm_w[layer][None, :], batch, tg)

    out = _out_call(oa, ob, p, x2, w_branch_a[layer].astype(BF16), w_branch_b[layer].astype(BF16),
                    w_out[layer].astype(BF16), final_norm_w[None, :], tm_out)
    return out.reshape(batch, seq, d)
```

```python
import functools
import math

import numpy as np
import jax
import jax.numpy as jnp
from jax import lax
from jax.experimental import pallas as pl
from jax.experimental.pallas import tpu as pltpu

F32 = jnp.float32
BF16 = jnp.bfloat16

N_META = 16
ROPE_THETA = 500000.0
NORM_EPS = 1e-5

DA_HEADS = 8
DA_HEAD_DIM = 64
DA_V_DIM = 128
VT_ROWS = DA_V_DIM + 16
DA_ROT_DIM = 16
GLA_HEADS = 4
GLA_DK = 128
GLA_DV = 256
GLA_GATE_RANK = 16
GLA_GATE_NORMALIZER = 16.0
GLA_CHUNK = 128

LANES = 128
VMEM_LIMIT = 48 * 1024 * 1024
ATTN_VMEM_LIMIT = 56 * 1024 * 1024

P_Q, P_K, P_Z, P_GQ, P_GK, P_GV, P_GZ, P_GA, P_GB = 0, 8, 16, 24, 28, 32, 40, 48, 56
P_COLS = 64 * LANES
PROJ_TN = 512
PROJ_RC = 256
W_IN_MID_COL = 3 * 1024

NEG_BIG = -1e30


def _nt(a, b):
    return lax.dot_general(a, b, (((1,), (1,)), ((), ())), preferred_element_type=F32)


def _tn(a, b):
    return lax.dot_general(a, b, (((0,), (0,)), ((), ())), preferred_element_type=F32)


def _nn(a, b):
    return jnp.dot(a, b, preferred_element_type=F32)


def _params(n_axes, vmem=VMEM_LIMIT):
    return pltpu.CompilerParams(dimension_semantics=("arbitrary",) * n_axes,
                                vmem_limit_bytes=vmem)


def _proj_kernel(x_ref, nw_ref, wqk_ref, wmid_ref, wgate_ref, cq_ref, sq_ref, ck_ref, sk_ref,
                 p_ref, u_ref, u_scr, *, rc):
    j = pl.program_id(1)
    tm = x_ref.shape[0]

    @pl.when(j == 0)
    def _():
        x = x_ref[...]
        ms = jnp.mean(x * x, axis=-1, keepdims=True)
        u = (x * lax.rsqrt(ms + NORM_EPS) * nw_ref[...]).astype(BF16)
        u_scr[...] = u
        u_ref[...] = u

    def run(w_ref, epilogue):
        for r in range(tm // rc):
            rows = slice(r * rc, (r + 1) * rc)
            y = _nn(u_scr[rows, :], w_ref[...])
            p_ref[rows, :] = epilogue(y, rows).astype(BF16)

    def rope(c_ref, s_ref):
        def epilogue(y, rows):
            c, s = c_ref[rows, :], s_ref[rows, :]
            outs = []
            for g in range(PROJ_TN // LANES):
                t = y[:, g * LANES:(g + 1) * LANES]
                outs.append(t * c + pltpu.roll(t, LANES // 2, 1) * s)
            return jnp.concatenate(outs, axis=1)
        return epilogue

    nq = (P_K - P_Q) * LANES // PROJ_TN
    nk = (P_Z - P_K) * LANES // PROJ_TN
    b_k, b_z, b_gq, b_gk = nq, nq + nk, P_GQ * LANES // PROJ_TN, P_GK * LANES // PROJ_TN
    b_gz, b_ga = P_GZ * LANES // PROJ_TN, P_GA * LANES // PROJ_TN

    @pl.when(j < b_k)
    def _():
        run(wqk_ref, rope(cq_ref, sq_ref))

    @pl.when((j >= b_k) & (j < b_z))
    def _():
        run(wqk_ref, rope(ck_ref, sk_ref))

    @pl.when(((j >= b_z) & (j < b_gq)) | ((j >= b_gz) & (j < b_ga)))
    def _():
        run(wmid_ref, lambda y, rows: y * jax.nn.sigmoid(y))

    @pl.when(j == b_gq)
    def _():
        run(wmid_ref, lambda y, rows: y * (GLA_DK ** -0.5))

    @pl.when((j >= b_gk) & (j < b_gz))
    def _():
        run(wmid_ref, lambda y, rows: y)

    @pl.when(j >= b_ga)
    def _():
        run(wgate_ref, lambda y, rows: jax.nn.sigmoid(y))


def _proj_call(x2, norm_w, w_qk, w_all, w_gate, tabs, tm):
    rows, d = x2.shape
    n_pos_blocks = tabs[0].shape[0] // tm
    grid = (rows // tm, P_COLS // PROJ_TN)
    n_qk = w_qk.shape[1] // PROJ_TN
    n_gate = w_gate.shape[1] // PROJ_TN
    mid0 = W_IN_MID_COL // PROJ_TN
    n_mid = grid[1] - n_qk - n_gate
    tab_spec = pl.BlockSpec((tm, LANES), lambda i, j: (i % n_pos_blocks, 0))
    kernel = functools.partial(_proj_kernel, rc=min(PROJ_RC, tm))
    return pl.pallas_call(
        kernel,
        grid=grid,
        in_specs=[
            pl.BlockSpec((tm, d), lambda i, j: (i, 0)),
            pl.BlockSpec((1, d), lambda i, j: (0, 0)),
            pl.BlockSpec((d, PROJ_TN), lambda i, j: (0, jnp.minimum(j, n_qk - 1))),
            pl.BlockSpec((d, PROJ_TN),
                         lambda i, j: (0, mid0 + jnp.clip(j - n_qk, 0, n_mid - 1))),
            pl.BlockSpec((d, PROJ_TN),
                         lambda i, j: (0, jnp.clip(j - n_qk - n_mid, 0, n_gate - 1))),
            tab_spec, tab_spec, tab_spec, tab_spec,
        ],
        out_specs=[
            pl.BlockSpec((tm, PROJ_TN), lambda i, j: (i, j)),
            pl.BlockSpec((tm, d), lambda i, j: (i, 0)),
        ],
        out_shape=[
            jax.ShapeDtypeStruct((rows, P_COLS), BF16),
            jax.ShapeDtypeStruct((rows, d), BF16),
        ],
        scratch_shapes=[pltpu.VMEM((tm, d), BF16)],
        compiler_params=_params(2),
        name="proj",
    )(x2, norm_w, w_qk, w_all, w_gate, *tabs)


def _vt_kernel(wvt_ref, ones_ref, u_ref, o_ref):
    o_ref[0, 0] = (_nt(wvt_ref[...], u_ref[...]) + ones_ref[...]).astype(BF16)


def _vt_call(wvt, ones_col, u, batch, tk):
    rows, d = u.shape
    n = wvt.shape[0]
    nblk = rows // batch // tk
    return pl.pallas_call(
        _vt_kernel,
        grid=(batch, nblk),
        in_specs=[
            pl.BlockSpec((n, d), lambda b, s: (0, 0)),
            pl.BlockSpec((n, 1), lambda b, s: (0, 0)),
            pl.BlockSpec((tk, d), lambda b, s: (b * nblk + s, 0)),
        ],
        out_specs=pl.BlockSpec((1, 1, n, tk), lambda b, s: (b, s, 0, 0)),
        out_shape=jax.ShapeDtypeStruct((batch, nblk, n, tk), BF16),
        compiler_params=_params(2),
        name="vt",
    )(wvt, ones_col, u)


def _lg_kernel(u_ref, wlr_ref, w2_ref, b_ref, o_ref):
    g_lr = _nn(u_ref[...], wlr_ref[...])
    gk = _nn(g_lr.astype(BF16), w2_ref[...]) + b_ref[...]
    log_sig = jnp.minimum(gk, 0.0) - jnp.log1p(jnp.exp(-jnp.abs(gk)))
    o_ref[...] = log_sig * (math.log2(math.e) / GLA_GATE_NORMALIZER)


def _lg_call(u, wlr, w2, bias, tm):
    rows, d = u.shape
    n = w2.shape[1]
    return pl.pallas_call(
        _lg_kernel,
        grid=(rows // tm,),
        in_specs=[
            pl.BlockSpec((tm, d), lambda i: (i, 0)),
            pl.BlockSpec(wlr.shape, lambda i: (0, 0)),
            pl.BlockSpec(w2.shape, lambda i: (0, 0)),
            pl.BlockSpec((1, n), lambda i: (0, 0)),
        ],
        out_specs=pl.BlockSpec((tm, n), lambda i: (i, 0)),
        out_shape=jax.ShapeDtypeStruct((rows, n), F32),
        compiler_params=_params(1),
        name="lg",
    )(u, wlr, w2, bias)


ATTN_GROUP = 2


def _attn_kernel(q_ref, k_ref, vt_ref, km_ref, vtm_ref, z_ref, map1_ref, bias_ref, lq1_ref,
                 lk1_ref, lq2_ref, lk2_ref, sw_ref, o_ref, qq_scr, s0_scr, s1_scr, sm_scr, m_scr,
                 mp_scr, acc_scr, *, tq, nq, lam_init):
    i = pl.program_id(2)
    w = 2 * tq
    grp = ATTN_GROUP
    cur = lax.rem(i, 2)
    prv = 1 - cur

    def fold_max(x):
        return jnp.max(x.reshape(x.shape[0] // 8, 8, w), axis=0)

    def p1_start():
        qb = q_ref[...]
        in_map1 = jnp.broadcast_to(map1_ref[...], qb.shape) > 0.5
        zero = jnp.zeros_like(qb)
        qq_scr[0:tq, :] = jnp.where(in_map1, qb, zero)
        qq_scr[tq:w, :] = jnp.where(in_map1, zero, qb)
        s_meta = _nt(km_ref[...], qq_scr[...])
        sm_scr[cur] = s_meta
        mp_scr[...] = fold_max(s_meta)

    def p1_block(s_p1, j, masked):
        kb = k_ref[pl.ds(pl.multiple_of(j * tq, tq), tq), :]
        s = _nt(kb, qq_scr[...])
        if masked:
            s = s + bias_ref[...]
        s_p1[j] = s
        return fold_max(s)

    def p1_blocks(s_p1, j0, n):
        for t in range(n):
            mp_scr[...] = jnp.maximum(mp_scr[...], p1_block(s_p1, j0 + t, False))

    def p1_finish(s_p1):
        mp = jnp.maximum(mp_scr[...], p1_block(s_p1, i, True))
        m_scr[cur] = jnp.max(mp, axis=0, keepdims=True)

    def p2_probs(s, m):
        return jnp.exp2(s - m).astype(BF16)

    def p2_blocks(s_p2, j0, n, m):
        for t in range(n):
            acc_scr[...] += _nn(vt_ref[0, j0 + t], p2_probs(s_p2[j0 + t], m))

    def p2_meta():
        acc_scr[...] += _nn(vtm_ref[...], p2_probs(sm_scr[prv], m_scr[prv]))

    def finish():
        lam = (jnp.exp(jnp.sum(lq1_ref[...] * lk1_ref[...], axis=-1, keepdims=True))
               - jnp.exp(jnp.sum(lq2_ref[...] * lk2_ref[...], axis=-1, keepdims=True))
               + lam_init)
        inv_l = 1.0 / acc_scr[DA_V_DIM:DA_V_DIM + 1, :]
        acc = acc_scr[0:DA_V_DIM, :]
        o = acc[:, 0:tq] * inv_l[:, 0:tq] - lam * (acc[:, tq:w] * inv_l[:, tq:w])
        ms = jnp.mean(o * o, axis=0, keepdims=True)
        y = o * lax.rsqrt(ms + NORM_EPS) * sw_ref[...] * (1.0 - lam_init)
        o_ref[...] = (y.T * z_ref[...].astype(F32)).astype(BF16)

    def middle(s_p1, s_p2):
        @pl.when(i < nq)
        def _():
            m_prev = m_scr[prv]
            n_grp = i // grp
            rem = i - n_grp * grp

            def body(g, carry):
                p1_blocks(s_p1, g * grp, grp)
                p2_blocks(s_p2, g * grp, grp, m_prev)
                return carry

            lax.fori_loop(0, n_grp, body, 0)
            for t in range(grp - 1):
                @pl.when(rem > t)
                def _():
                    p1_blocks(s_p1, n_grp * grp + t, 1)
                    p2_blocks(s_p2, n_grp * grp + t, 1, m_prev)
            p1_finish(s_p1)
            p2_meta()

        @pl.when(i == nq)
        def _():
            m_prev = m_scr[prv]

            def body(g, carry):
                p2_blocks(s_p2, g * grp, grp, m_prev)
                return carry

            lax.fori_loop(0, nq // grp, body, 0)
            for t in range(nq % grp):
                p2_blocks(s_p2, (nq // grp) * grp + t, 1, m_prev)
            p2_meta()

    @pl.when(i == 0)
    def _():
        m_scr[...] = jnp.zeros_like(m_scr)
        sm_scr[...] = jnp.zeros_like(sm_scr)
        acc_scr[...] = jnp.ones_like(acc_scr)

    finish()
    acc_scr[...] = jnp.zeros_like(acc_scr)
    p1_start()

    @pl.when(cur == 0)
    def _():
        middle(s0_scr, s1_scr)

    @pl.when(cur == 1)
    def _():
        middle(s1_scr, s0_scr)


def _attn_call(p, vt, pm, vtm, map1_lanes, lam_vecs, subln_w, batch, tq, lam_init):
    rows = p.shape[0]
    seq = rows // batch
    nq = seq // tq
    small = pl.BlockSpec((1, DA_HEAD_DIM), lambda b, h, i: (0, 0))
    key = np.arange(tq)[:, None]
    qry = np.arange(2 * tq)[None, :] % tq
    bias = jnp.asarray(np.where(key <= qry, 0.0, NEG_BIG), F32)
    kernel = functools.partial(_attn_kernel, tq=tq, nq=nq, lam_init=lam_init)
    p1_tile = lambda b, i: b * nq + jnp.minimum(i, nq - 1)
    out_tile = lambda b, i: b * nq + jnp.clip(i - 2, 0, nq - 1)
    return pl.pallas_call(
        kernel,
        grid=(batch, DA_HEADS, nq + 2),
        in_specs=[
            pl.BlockSpec((tq, LANES), lambda b, h, i: (p1_tile(b, i), P_Q + h)),
            pl.BlockSpec((seq, LANES), lambda b, h, i: (b, P_K + h)),
            pl.BlockSpec((1, nq, VT_ROWS, tq), lambda b, h, i: (b, 0, h, 0)),
            pl.BlockSpec((N_META, LANES), lambda b, h, i: (0, P_K + h)),
            pl.BlockSpec((VT_ROWS, N_META), lambda b, h, i: (h, 0)),
            pl.BlockSpec((tq, LANES), lambda b, h, i: (out_tile(b, i), P_Z + h)),
            pl.BlockSpec((1, LANES), lambda b, h, i: (0, 0)),
            pl.BlockSpec((tq, 2 * tq), lambda b, h, i: (0, 0)),
            small, small, small, small,
            pl.BlockSpec((DA_V_DIM, 1), lambda b, h, i: (0, 0)),
        ],
        out_specs=pl.BlockSpec((tq, LANES), lambda b, h, i: (out_tile(b, i), h)),
        out_shape=jax.ShapeDtypeStruct((rows, DA_HEADS * DA_V_DIM), BF16),
        scratch_shapes=[
            pltpu.VMEM((2 * tq, LANES), BF16),
            pltpu.VMEM((nq, tq, 2 * tq), F32),
            pltpu.VMEM((nq, tq, 2 * tq), F32),
            pltpu.VMEM((2, N_META, 2 * tq), F32),
            pltpu.VMEM((2, 1, 2 * tq), F32),
            pltpu.VMEM((8, 2 * tq), F32),
            pltpu.VMEM((VT_ROWS, 2 * tq), F32),
        ],
        compiler_params=_params(3, ATTN_VMEM_LIMIT),
        name="attn",
    )(p, p, vt, pm, vtm, p, map1_lanes, bias, *lam_vecs, subln_w)


_GLA_LEVELS = tuple(GLA_CHUNK >> (i + 1) for i in range(GLA_CHUNK.bit_length() - 1))


def _gla_constants():
    c = GLA_CHUNK
    t = np.arange(c)[:, None]
    i = np.arange(c)[None, :]
    mats, masks = [], []
    for s in _GLA_LEVELS:
        upper = (t % (2 * s)) >= s
        r = (t // (2 * s)) * (2 * s) + s - 1
        mats.append(np.where(upper, (i > r) & (i <= t), (i > t) & (i <= r)))
        masks.append(upper & ~upper.T & ((t // (2 * s)) == (i // (2 * s))))
    mats.append(i <= t)
    mats.append(i > t)
    masks.append(t == i)
    m_all = np.concatenate(mats, axis=0).astype(np.float32)
    mask_all = np.stack(masks, axis=0).astype(np.float32)
    tm = np.arange(N_META)
    m_meta = (tm[None, :] > tm[:, None]).astype(np.float32)
    twice = lambda m: np.concatenate([m, m], axis=1)
    return twice(m_all), mask_all, twice(m_meta)


def _gate_factors(mat2, lg2):
    hi = lg2.astype(BF16)
    lo = (lg2 - hi.astype(F32)).astype(BF16)
    return jnp.exp2(_nn(mat2, jnp.concatenate([hi, lo], axis=0)))


def _gla_kernel(q_ref, k_ref, v_ref, lg_ref, z_ref, km_ref, vm_ref, lgm_ref, mall_ref,
                mask_ref, mmeta_ref, nw_ref, o_ref, st_scr, *, tg):
    sblk = pl.program_id(1)
    c = GLA_CHUNK
    nlev = len(_GLA_LEVELS)

    @pl.when(sblk == 0)
    def _():
        f_meta = _gate_factors(mmeta_ref[...], lgm_ref[...])
        for h in range(GLA_HEADS):
            ksl = slice(h * GLA_DK, (h + 1) * GLA_DK)
            vsl = slice(h * GLA_DV, (h + 1) * GLA_DV)
            kt = (km_ref[:, ksl].astype(F32) * f_meta[:, ksl]).astype(BF16)
            st_scr[h] = _tn(vm_ref[:, vsl], kt)

    def chunk(ci, carry):
        r0 = pl.multiple_of(ci * c, c)
        f_all = _gate_factors(mall_ref[...], lg_ref[pl.ds(r0, c), :])
        for h in range(GLA_HEADS):
            ksl = slice(h * GLA_DK, (h + 1) * GLA_DK)
            vsl = slice(h * GLA_DV, (h + 1) * GLA_DV)
            q = q_ref[pl.ds(r0, c), ksl]
            k = k_ref[pl.ds(r0, c), ksl]
            v = v_ref[pl.ds(r0, c), vsl]
            qf = q.astype(F32)
            kf = k.astype(F32)
            a = jnp.where(mask_ref[nlev] > 0.5, _nt(q, k), 0.0)
            for lv in range(nlev):
                fl = f_all[lv * c:(lv + 1) * c, ksl]
                a = jnp.where(mask_ref[lv] > 0.5,
                              _nt((qf * fl).astype(BF16), (kf * fl).astype(BF16)), a)
            fb = f_all[nlev * c:(nlev + 1) * c, ksl]
            fk = f_all[(nlev + 1) * c:(nlev + 2) * c, ksl]
            st = st_scr[h]
            o = _nn(a.astype(BF16), v) + _nt((qf * fb).astype(BF16), st.astype(BF16))
            st_scr[h] = st * fb[c - 1:c, :] + _tn(v, (kf * fk).astype(BF16))
            ms = jnp.mean(o * o, axis=-1, keepdims=True)
            y = o * lax.rsqrt(ms + NORM_EPS) * nw_ref[...]
            o_ref[pl.ds(r0, c), vsl] = (y * z_ref[pl.ds(r0, c), vsl].astype(F32)).astype(BF16)
        return carry

    lax.fori_loop(0, tg // c, chunk, 0)


def _gla_call(p, lg, pm, lgm, norm_w, batch, tg):
    rows = p.shape[0]
    nblk = rows // batch // tg
    m_all, mask_all, m_meta = _gla_constants()
    kw, vw = GLA_HEADS * GLA_DK, GLA_HEADS * GLA_DV
    row = lambda b, s: b * nblk + s
    const2 = lambda b, s: (0, 0)
    kernel = functools.partial(_gla_kernel, tg=tg)
    return pl.pallas_call(
        kernel,
        grid=(batch, nblk),
        in_specs=[
            pl.BlockSpec((tg, kw), lambda b, s: (row(b, s), P_GQ * LANES // kw)),
            pl.BlockSpec((tg, kw), lambda b, s: (row(b, s), P_GK * LANES // kw)),
            pl.BlockSpec((tg, vw), lambda b, s: (row(b, s), P_GV * LANES // vw)),
            pl.BlockSpec((tg, kw), lambda b, s: (row(b, s), 0)),
            pl.BlockSpec((tg, vw), lambda b, s: (row(b, s), P_GZ * LANES // vw)),
            pl.BlockSpec((N_META, kw), lambda b, s: (0, P_GK * LANES // kw)),
            pl.BlockSpec((N_META, vw), lambda b, s: (0, P_GV * LANES // vw)),
            pl.BlockSpec((N_META, kw), const2),
            pl.BlockSpec(m_all.shape, const2),
            pl.BlockSpec(mask_all.shape, lambda b, s: (0, 0, 0)),
            pl.BlockSpec(m_meta.shape, const2),
            pl.BlockSpec((1, GLA_DV), const2),
        ],
        out_specs=pl.BlockSpec((tg, vw), lambda b, s: (row(b, s), 0)),
        out_shape=jax.ShapeDtypeStruct((rows, vw), BF16),
        scratch_shapes=[pltpu.VMEM((GLA_HEADS, GLA_DV, GLA_DK), F32)],
        compiler_params=_params(2),
        name="gla",
    )(p, p, p, lg, p, pm, pm, lgm, jnp.asarray(m_all, BF16), jnp.asarray(mask_all, F32),
      jnp.asarray(m_meta, BF16), norm_w)


def _out_kernel(oa_ref, ob_ref, ga_ref, gb_ref, x_ref, wa_ref, wb_ref, wo_ref, fw_ref, o_ref):
    ya = _nn(oa_ref[...], wa_ref[...])
    yb = _nn(ob_ref[...], wb_ref[...])
    merged = ga_ref[...].astype(F32) * ya + gb_ref[...].astype(F32) * yb
    hid = x_ref[...] + _nn(merged.astype(BF16), wo_ref[...])
    ms = jnp.mean(hid * hid, axis=-1, keepdims=True)
    o_ref[...] = hid * lax.rsqrt(ms + NORM_EPS) * fw_ref[...]


def _out_call(oa, ob, p, x2, wa, wb, wo, fw, tm):
    rows, d = x2.shape
    rowblk = lambda i: (i, 0)
    const = lambda i: (0, 0)
    return pl.pallas_call(
        _out_kernel,
        grid=(rows // tm,),
        in_specs=[
            pl.BlockSpec((tm, d), rowblk),
            pl.BlockSpec((tm, d), rowblk),
            pl.BlockSpec((tm, d), lambda i: (i, P_GA * LANES // d)),
            pl.BlockSpec((tm, d), lambda i: (i, P_GB * LANES // d)),
            pl.BlockSpec((tm, d), rowblk),
            pl.BlockSpec((d, d), const),
            pl.BlockSpec((d, d), const),
            pl.BlockSpec((d, d), const),
            pl.BlockSpec((1, d), const),
        ],
        out_specs=pl.BlockSpec((tm, d), rowblk),
        out_shape=jax.ShapeDtypeStruct((rows, d), F32),
        compiler_params=_params(1),
        name="out",
    )(oa, ob, p, p, x2, wa, wb, wo, fw)


def _head_lane_layout():
    half_r = DA_ROT_DIM // 2
    n_plain = (DA_HEAD_DIM - DA_ROT_DIM) // 2
    runs, is_map1 = [], []
    for half in range(2):
        rot0 = half * half_r
        plain0 = DA_ROT_DIM + half * n_plain
        for which, d0, n in ((0, rot0, half_r), (1, rot0, half_r),
                             (0, plain0, n_plain), (1, plain0, n_plain)):
            runs.append((which, d0, n))
            is_map1 += [which == 0] * n
    return runs, np.array(is_map1)


def _permute_head_columns(w_cols, runs):
    d = w_cols.shape[0]
    w4 = w_cols.reshape(d, DA_HEADS, 2, DA_HEAD_DIM)
    pieces = [w4[:, :, which, d0:d0 + n] for which, d0, n in runs]
    return jnp.concatenate(pieces, axis=-1).reshape(d, DA_HEADS * 2 * DA_HEAD_DIM)


def _rope_tables(pos, scale):
    half_r = DA_ROT_DIM // 2
    lane = np.arange(LANES) % (LANES // 2)
    rotary = lane < DA_ROT_DIM
    inv_freq = ROPE_THETA ** (-jnp.arange(half_r, dtype=F32) / half_r)
    inv_lane = jnp.where(jnp.asarray(rotary), inv_freq[lane % half_r], 0.0)
    sign = np.where(rotary, np.where(np.arange(LANES) < LANES // 2, -1.0, 1.0), 0.0)
    ang = pos.astype(F32)[:, None] * inv_lane[None, :]
    return jnp.cos(ang) * scale, jnp.sin(ang) * jnp.asarray(sign * scale, F32)[None, :]


def kernel(x, meta_tokens, norm_w, w_in, lam_q1, lam_k1, lam_q2, lam_k2, da_subln_w, gla_gate_w2,
           gla_gate_b, gla_norm_w, w_branch_a, w_branch_b, w_out, final_norm_w):
    batch, seq, d = x.shape
    depth = norm_w.shape[0]
    assert depth == 1
    layer = 0
    lam_init = 0.8 - 0.6 * math.exp(-0.3 * layer)
    rows = batch * seq
    x2 = x.reshape(rows, d)

    w_all = w_in[layer].astype(BF16)
    c = np.cumsum([0, 1024, 1024, 1024, 1024, 512, 512, 1024, 1024, GLA_GATE_RANK, 1024, 1024])
    sl = lambda a: w_all[:, c[a]:c[a + 1]]
    runs, lane_is_map1 = _head_lane_layout()
    w_qk = jnp.concatenate([_permute_head_columns(sl(0), runs),
                            _permute_head_columns(sl(1), runs)], axis=1)
    w_gate = w_all[:, c[9]:c[11]]
    wvt = jnp.pad(sl(2).T.reshape(DA_HEADS, DA_V_DIM, d),
                  ((0, 0), (0, VT_ROWS - DA_V_DIM), (0, 0))).reshape(DA_HEADS * VT_ROWS, d)
    ones_col = jnp.asarray((np.arange(DA_HEADS * VT_ROWS) % VT_ROWS == DA_V_DIM)[:, None], F32)
    wlr = jnp.pad(sl(8), ((0, 0), (0, LANES - GLA_GATE_RANK)))
    w2 = jnp.pad(gla_gate_w2[layer], ((0, LANES - GLA_GATE_RANK), (0, 0))).astype(BF16)
    gate_b = gla_gate_b[layer][None, :]
    nw = norm_w[layer][None, :]

    q_scale = DA_HEAD_DIM ** -0.5 * math.log2(math.e)
    pos_meta = jnp.arange(N_META, dtype=jnp.int32)
    pos_real = jnp.arange(N_META, N_META + seq, dtype=jnp.int32)
    tabs_meta = _rope_tables(pos_meta, q_scale) + _rope_tables(pos_meta, 1.0)
    tabs_real = _rope_tables(pos_real, q_scale) + _rope_tables(pos_real, 1.0)
    map1_lanes = jnp.asarray(lane_is_map1[None, :], F32)

    tm_proj = min(1024, seq)
    tq = min(512, seq)
    tg = min(512, seq)
    tm_out = min(512, seq)

    pm, um = _proj_call(meta_tokens.astype(F32), nw, w_qk, w_all, w_gate, tabs_meta, tm=N_META)
    vtm = _vt_call(wvt, ones_col, um, 1, N_META)[0, 0]
    lgm = _lg_call(um, wlr, w2, gate_b, N_META)

    p, u = _proj_call(x2, nw, w_qk, w_all, w_gate, tabs_real, tm=tm_proj)
    vt = _vt_call(wvt, ones_col, u, batch, tq)
    lg = _lg_call(u, wlr, w2, gate_b, tm_proj)

    lam_vecs = [v[layer][None, :] for v in (lam_q1, lam_k1, lam_q2, lam_k2)]
    oa = _attn_call(p, vt, pm, vtm, map1_lanes, lam_vecs, da_subln_w[layer][:, None], batch, tq,
                    lam_init)
    ob = _gla_call(p, lg, pm, lgm, gla_norm_w[layer][None, :], batch, tg)

    out = _out_call(oa, ob, p, x2, w_branch_a[layer].astype(BF16), w_branch_b[layer].astype(BF16),
                    w_out[layer].astype(BF16), final_norm_w[None, :], tm_out)
    return out.reshape(batch, seq, d)
```

```python
import functools
import math

import numpy as np
import jax
import jax.numpy as jnp
from jax import lax
from jax.experimental import pallas as pl
from jax.experimental.pallas import tpu as pltpu

F32 = jnp.float32
BF16 = jnp.bfloat16

N_META = 16
ROPE_THETA = 500000.0
NORM_EPS = 1e-5

DA_HEADS = 8
DA_HEAD_DIM = 64
DA_V_DIM = 128
VT_ROWS = DA_V_DIM + 16
DA_ROT_DIM = 16
GLA_HEADS = 4
GLA_DK = 128
GLA_DV = 256
GLA_GATE_RANK = 16
GLA_GATE_NORMALIZER = 16.0
GLA_CHUNK = 128

LANES = 128
VMEM_LIMIT = 48 * 1024 * 1024
ATTN_VMEM_LIMIT = 56 * 1024 * 1024

P_Q, P_K, P_Z, P_GQ, P_GK, P_GV, P_GZ, P_GA, P_GB = 0, 8, 16, 24, 28, 32, 40, 48, 56
P_COLS = 64 * LANES
PROJ_TN = 1024
PROJ_RC = 256

NEG_BIG = -1e30


def _nt(a, b):
    return lax.dot_general(a, b, (((1,), (1,)), ((), ())), preferred_element_type=F32)


def _tn(a, b):
    return lax.dot_general(a, b, (((0,), (0,)), ((), ())), preferred_element_type=F32)


def _nn(a, b):
    return jnp.dot(a, b, preferred_element_type=F32)


def _params(n_axes, vmem=VMEM_LIMIT):
    return pltpu.CompilerParams(dimension_semantics=("arbitrary",) * n_axes,
                                vmem_limit_bytes=vmem)


_P_UNIT = 512
_P_UNIT_KINDS = ("rope_q", "rope_q", "rope_k", "rope_k", "silu", "silu", "scale_gq", "id",
                 "id", "id", "silu", "silu", "sigmoid", "sigmoid", "sigmoid", "sigmoid")


def _proj_kernel(x_ref, nw_ref, wqk_ref, wmid_ref, wgate_ref, cq_ref, sq_ref, ck_ref, sk_ref,
                 p_ref, u_ref, u_scr, *, rc, tn, n_qk, n_mid):
    j = pl.program_id(1)
    tm = x_ref.shape[0]

    @pl.when(j == 0)
    def _():
        x = x_ref[...]
        ms = jnp.mean(x * x, axis=-1, keepdims=True)
        u = (x * lax.rsqrt(ms + NORM_EPS) * nw_ref[...]).astype(BF16)
        u_scr[...] = u
        u_ref[...] = u

    def rope(t, c, s):
        outs = [t[:, g:g + LANES] * c + pltpu.roll(t[:, g:g + LANES], LANES // 2, 1) * s
                for g in range(0, t.shape[1], LANES)]
        return jnp.concatenate(outs, axis=1)

    def unit_epilogue(kind, t, rows):
        if kind == "rope_q":
            return rope(t, cq_ref[rows, :], sq_ref[rows, :])
        if kind == "rope_k":
            return rope(t, ck_ref[rows, :], sk_ref[rows, :])
        if kind == "silu":
            return t * jax.nn.sigmoid(t)
        if kind == "sigmoid":
            return jax.nn.sigmoid(t)
        if kind == "scale_gq":
            return t * (GLA_DK ** -0.5)
        return t

    def run(w_ref, kinds):
        for r in range(tm // rc):
            rows = slice(r * rc, (r + 1) * rc)
            y = _nn(u_scr[rows, :], w_ref[...])
            outs = [unit_epilogue(kind, y[:, n * _P_UNIT:(n + 1) * _P_UNIT], rows)
                    for n, kind in enumerate(kinds)]
            p_ref[rows, :] = jnp.concatenate(outs, axis=1).astype(BF16)

    units = tn // _P_UNIT
    n_tiles = P_COLS // tn
    tile_kinds = [_P_UNIT_KINDS[t * units:(t + 1) * units] for t in range(n_tiles)]
    source = lambda t: wqk_ref if t < n_qk else (wmid_ref if t < n_qk + n_mid else wgate_ref)
    for kinds in sorted(set(tile_kinds)):
        tiles = [t for t in range(n_tiles) if tile_kinds[t] == kinds]
        w_ref = source(tiles[0])
        assert all(source(t) is w_ref for t in tiles)
        cond = functools.reduce(lambda a, b: a | b, [j == t for t in tiles])
        pl.when(cond)(functools.partial(run, w_ref, kinds))


def _proj_call(x2, norm_w, w_qk, w_mid, w_gate, tabs, tm):
    rows, d = x2.shape
    tn = PROJ_TN
    n_pos_blocks = tabs[0].shape[0] // tm
    grid = (rows // tm, P_COLS // tn)
    n_qk, n_mid, n_gate = w_qk.shape[1] // tn, w_mid.shape[1] // tn, w_gate.shape[1] // tn
    assert n_qk + n_mid + n_gate == grid[1]
    tab_spec = pl.BlockSpec((tm, LANES), lambda i, j: (i % n_pos_blocks, 0))
    kernel = functools.partial(_proj_kernel, rc=min(PROJ_RC, tm), tn=tn, n_qk=n_qk, n_mid=n_mid)
    return pl.pallas_call(
        kernel,
        grid=grid,
        in_specs=[
            pl.BlockSpec((tm, d), lambda i, j: (i, 0)),
            pl.BlockSpec((1, d), lambda i, j: (0, 0)),
            pl.BlockSpec((d, tn), lambda i, j: (0, jnp.minimum(j, n_qk - 1))),
            pl.BlockSpec((d, tn), lambda i, j: (0, jnp.clip(j - n_qk, 0, n_mid - 1))),
            pl.BlockSpec((d, tn), lambda i, j: (0, jnp.clip(j - n_qk - n_mid, 0, n_gate - 1))),
            tab_spec, tab_spec, tab_spec, tab_spec,
        ],
        out_specs=[
            pl.BlockSpec((tm, tn), lambda i, j: (i, j)),
            pl.BlockSpec((tm, d), lambda i, j: (i, 0)),
        ],
        out_shape=[
            jax.ShapeDtypeStruct((rows, P_COLS), BF16),
            jax.ShapeDtypeStruct((rows, d), BF16),
        ],
        scratch_shapes=[pltpu.VMEM((tm, d), BF16)],
        compiler_params=_params(2),
        name="proj",
    )(x2, norm_w, w_qk, w_mid, w_gate, *tabs)


def _vt_kernel(wvt_ref, ones_ref, u_ref, o_ref):
    o_ref[0, 0] = (_nt(wvt_ref[...], u_ref[...]) + ones_ref[...]).astype(BF16)


def _vt_call(wvt, ones_col, u, batch, tk):
    rows, d = u.shape
    n = wvt.shape[0]
    nblk = rows // batch // tk
    return pl.pallas_call(
        _vt_kernel,
        grid=(batch, nblk),
        in_specs=[
            pl.BlockSpec((n, d), lambda b, s: (0, 0)),
            pl.BlockSpec((n, 1), lambda b, s: (0, 0)),
            pl.BlockSpec((tk, d), lambda b, s: (b * nblk + s, 0)),
        ],
        out_specs=pl.BlockSpec((1, 1, n, tk), lambda b, s: (b, s, 0, 0)),
        out_shape=jax.ShapeDtypeStruct((batch, nblk, n, tk), BF16),
        compiler_params=_params(2),
        name="vt",
    )(wvt, ones_col, u)


def _lg_kernel(u_ref, wlr_ref, w2_ref, b_ref, o_ref):
    g_lr = _nn(u_ref[...], wlr_ref[...])
    gk = _nn(g_lr.astype(BF16), w2_ref[...]) + b_ref[...]
    log_sig = jnp.minimum(gk, 0.0) - jnp.log1p(jnp.exp(-jnp.abs(gk)))
    o_ref[...] = log_sig * (math.log2(math.e) / GLA_GATE_NORMALIZER)


def _lg_call(u, wlr, w2, bias, tm):
    rows, d = u.shape
    n = w2.shape[1]
    return pl.pallas_call(
        _lg_kernel,
        grid=(rows // tm,),
        in_specs=[
            pl.BlockSpec((tm, d), lambda i: (i, 0)),
            pl.BlockSpec(wlr.shape, lambda i: (0, 0)),
            pl.BlockSpec(w2.shape, lambda i: (0, 0)),
            pl.BlockSpec((1, n), lambda i: (0, 0)),
        ],
        out_specs=pl.BlockSpec((tm, n), lambda i: (i, 0)),
        out_shape=jax.ShapeDtypeStruct((rows, n), F32),
        compiler_params=_params(1),
        name="lg",
    )(u, wlr, w2, bias)


ATTN_GROUP = 2


def _attn_kernel(q_ref, k_ref, vt_ref, km_ref, vtm_ref, z_ref, map1_ref, bias_ref, lq1_ref,
                 lk1_ref, lq2_ref, lk2_ref, sw_ref, o_ref, qq_scr, s0_scr, s1_scr, sm_scr, m_scr,
                 mp_scr, acc_scr, *, tq, nq, lam_init):
    i = pl.program_id(2)
    w = 2 * tq
    grp = ATTN_GROUP
    cur = lax.rem(i, 2)
    prv = 1 - cur

    def fold_max(x):
        return jnp.max(x.reshape(x.shape[0] // 8, 8, w), axis=0)

    def p1_start():
        qb = q_ref[...]
        in_map1 = jnp.broadcast_to(map1_ref[...], qb.shape) > 0.5
        zero = jnp.zeros_like(qb)
        qq_scr[0:tq, :] = jnp.where(in_map1, qb, zero)
        qq_scr[tq:w, :] = jnp.where(in_map1, zero, qb)
        s_meta = _nt(km_ref[...], qq_scr[...])
        sm_scr[cur] = s_meta
        mp_scr[...] = fold_max(s_meta)

    def p1_block(s_p1, j, masked):
        kb = k_ref[pl.ds(pl.multiple_of(j * tq, tq), tq), :]
        s = _nt(kb, qq_scr[...])
        if masked:
            s = s + bias_ref[...]
        s_p1[j] = s
        return fold_max(s)

    def p1_blocks(s_p1, j0, n):
        for t in range(n):
            mp_scr[...] = jnp.maximum(mp_scr[...], p1_block(s_p1, j0 + t, False))

    def p1_finish(s_p1):
        mp = jnp.maximum(mp_scr[...], p1_block(s_p1, i, True))
        m_scr[cur] = jnp.max(mp, axis=0, keepdims=True)

    def p2_probs(s, m):
        return jnp.exp2(s - m).astype(BF16)

    def p2_blocks(s_p2, j0, n, m):
        for t in range(n):
            acc_scr[...] += _nn(vt_ref[0, j0 + t], p2_probs(s_p2[j0 + t], m))

    def p2_meta():
        acc_scr[...] += _nn(vtm_ref[...], p2_probs(sm_scr[prv], m_scr[prv]))

    def finish():
        lam = (jnp.exp(jnp.sum(lq1_ref[...] * lk1_ref[...], axis=-1, keepdims=True))
               - jnp.exp(jnp.sum(lq2_ref[...] * lk2_ref[...], axis=-1, keepdims=True))
               + lam_init)
        inv_l = 1.0 / acc_scr[DA_V_DIM:DA_V_DIM + 1, :]
        acc = acc_scr[0:DA_V_DIM, :]
        o = acc[:, 0:tq] * inv_l[:, 0:tq] - lam * (acc[:, tq:w] * inv_l[:, tq:w])
        ms = jnp.mean(o * o, axis=0, keepdims=True)
        y = o * lax.rsqrt(ms + NORM_EPS) * sw_ref[...] * (1.0 - lam_init)
        o_ref[...] = (y.T * z_ref[...].astype(F32)).astype(BF16)

    def middle(s_p1, s_p2):
        @pl.when(i < nq)
        def _():
            m_prev = m_scr[prv]
            n_grp = i // grp
            rem = i - n_grp * grp

            def body(g, carry):
                p1_blocks(s_p1, g * grp, grp)
                p2_blocks(s_p2, g * grp, grp, m_prev)
                return carry

            lax.fori_loop(0, n_grp, body, 0)
            for t in range(grp - 1):
                @pl.when(rem > t)
                def _():
                    p1_blocks(s_p1, n_grp * grp + t, 1)
                    p2_blocks(s_p2, n_grp * grp + t, 1, m_prev)
            p1_finish(s_p1)
            p2_meta()

        @pl.when(i == nq)
        def _():
            m_prev = m_scr[prv]

            def body(g, carry):
                p2_blocks(s_p2, g * grp, grp, m_prev)
                return carry

            lax.fori_loop(0, nq // grp, body, 0)
            for t in range(nq % grp):
                p2_blocks(s_p2, (nq // grp) * grp + t, 1, m_prev)
            p2_meta()

    @pl.when(i == 0)
    def _():
        m_scr[...] = jnp.zeros_like(m_scr)
        sm_scr[...] = jnp.zeros_like(sm_scr)
        acc_scr[...] = jnp.ones_like(acc_scr)

    finish()
    acc_scr[...] = jnp.zeros_like(acc_scr)
    p1_start()

    @pl.when(cur == 0)
    def _():
        middle(s0_scr, s1_scr)

    @pl.when(cur == 1)
    def _():
        middle(s1_scr, s0_scr)


def _attn_call(p, vt, pm, vtm, map1_lanes, lam_vecs, subln_w, batch, tq, lam_init):
    rows = p.shape[0]
    seq = rows // batch
    nq = seq // tq
    small = pl.BlockSpec((1, DA_HEAD_DIM), lambda b, h, i: (0, 0))
    key = np.arange(tq)[:, None]
    qry = np.arange(2 * tq)[None, :] % tq
    bias = jnp.asarray(np.where(key <= qry, 0.0, NEG_BIG), F32)
    kernel = functools.partial(_attn_kernel, tq=tq, nq=nq, lam_init=lam_init)
    p1_tile = lambda b, i: b * nq + jnp.minimum(i, nq - 1)
    out_tile = lambda b, i: b * nq + jnp.clip(i - 2, 0, nq - 1)
    return pl.pallas_call(
        kernel,
        grid=(batch, DA_HEADS, nq + 2),
        in_specs=[
            pl.BlockSpec((tq, LANES), lambda b, h, i: (p1_tile(b, i), P_Q + h)),
            pl.BlockSpec((seq, LANES), lambda b, h, i: (b, P_K + h)),
            pl.BlockSpec((1, nq, VT_ROWS, tq), lambda b, h, i: (b, 0, h, 0)),
            pl.BlockSpec((N_META, LANES), lambda b, h, i: (0, P_K + h)),
            pl.BlockSpec((VT_ROWS, N_META), lambda b, h, i: (h, 0)),
            pl.BlockSpec((tq, LANES), lambda b, h, i: (out_tile(b, i), P_Z + h)),
            pl.BlockSpec((1, LANES), lambda b, h, i: (0, 0)),
            pl.BlockSpec((tq, 2 * tq), lambda b, h, i: (0, 0)),
            small, small, small, small,
            pl.BlockSpec((DA_V_DIM, 1), lambda b, h, i: (0, 0)),
        ],
        out_specs=pl.BlockSpec((tq, LANES), lambda b, h, i: (out_tile(b, i), h)),
        out_shape=jax.ShapeDtypeStruct((rows, DA_HEADS * DA_V_DIM), BF16),
        scratch_shapes=[
            pltpu.VMEM((2 * tq, LANES), BF16),
            pltpu.VMEM((nq, tq, 2 * tq), F32),
            pltpu.VMEM((nq, tq, 2 * tq), F32),
            pltpu.VMEM((2, N_META, 2 * tq), F32),
            pltpu.VMEM((2, 1, 2 * tq), F32),
            pltpu.VMEM((8, 2 * tq), F32),
            pltpu.VMEM((VT_ROWS, 2 * tq), F32),
        ],
        compiler_params=_params(3, ATTN_VMEM_LIMIT),
        name="attn",
    )(p, p, vt, pm, vtm, p, map1_lanes, bias, *lam_vecs, subln_w)


_GLA_LEVELS = tuple(GLA_CHUNK >> (i + 1) for i in range(GLA_CHUNK.bit_length() - 1))


def _gla_constants():
    c = GLA_CHUNK
    t = np.arange(c)[:, None]
    i = np.arange(c)[None, :]
    mats, masks = [], []
    for s in _GLA_LEVELS:
        upper = (t % (2 * s)) >= s
        r = (t // (2 * s)) * (2 * s) + s - 1
        mats.append(np.where(upper, (i > r) & (i <= t), (i > t) & (i <= r)))
        masks.append(upper & ~upper.T & ((t // (2 * s)) == (i // (2 * s))))
    mats.append(i <= t)
    mats.append(i > t)
    masks.append(t == i)
    m_all = np.concatenate(mats, axis=0).astype(np.float32)
    mask_all = np.stack(masks, axis=0).astype(np.float32)
    tm = np.arange(N_META)
    m_meta = (tm[None, :] > tm[:, None]).astype(np.float32)
    twice = lambda m: np.concatenate([m, m], axis=1)
    return twice(m_all), mask_all, twice(m_meta)


def _gate_factors(mat2, lg2):
    hi = lg2.astype(BF16)
    lo = (lg2 - hi.astype(F32)).astype(BF16)
    return jnp.exp2(_nn(mat2, jnp.concatenate([hi, lo], axis=0)))


def _gla_kernel(q_ref, k_ref, v_ref, lg_ref, z_ref, km_ref, vm_ref, lgm_ref, mall_ref,
                mask_ref, mmeta_ref, nw_ref, o_ref, st_scr, *, tg):
    sblk = pl.program_id(1)
    c = GLA_CHUNK
    nlev = len(_GLA_LEVELS)

    @pl.when(sblk == 0)
    def _():
        f_meta = _gate_factors(mmeta_ref[...], lgm_ref[...])
        for h in range(GLA_HEADS):
            ksl = slice(h * GLA_DK, (h + 1) * GLA_DK)
            vsl = slice(h * GLA_DV, (h + 1) * GLA_DV)
            kt = (km_ref[:, ksl].astype(F32) * f_meta[:, ksl]).astype(BF16)
            st_scr[h] = _tn(vm_ref[:, vsl], kt)

    def chunk(ci, carry):
        r0 = pl.multiple_of(ci * c, c)
        f_all = _gate_factors(mall_ref[...], lg_ref[pl.ds(r0, c), :])
        for h in range(GLA_HEADS):
            ksl = slice(h * GLA_DK, (h + 1) * GLA_DK)
            vsl = slice(h * GLA_DV, (h + 1) * GLA_DV)
            q = q_ref[pl.ds(r0, c), ksl]
            k = k_ref[pl.ds(r0, c), ksl]
            v = v_ref[pl.ds(r0, c), vsl]
            qf = q.astype(F32)
            kf = k.astype(F32)
            a = jnp.where(mask_ref[nlev] > 0.5, _nt(q, k), 0.0)
            for lv in range(nlev):
                fl = f_all[lv * c:(lv + 1) * c, ksl]
                a = jnp.where(mask_ref[lv] > 0.5,
                              _nt((qf * fl).astype(BF16), (kf * fl).astype(BF16)), a)
            fb = f_all[nlev * c:(nlev + 1) * c, ksl]
            fk = f_all[(nlev + 1) * c:(nlev + 2) * c, ksl]
            st = st_scr[h]
            o = _nn(a.astype(BF16), v) + _nt((qf * fb).astype(BF16), st.astype(BF16))
            st_scr[h] = st * fb[c - 1:c, :] + _tn(v, (kf * fk).astype(BF16))
            ms = jnp.mean(o * o, axis=-1, keepdims=True)
            y = o * lax.rsqrt(ms + NORM_EPS) * nw_ref[...]
            o_ref[pl.ds(r0, c), vsl] = (y * z_ref[pl.ds(r0, c), vsl].astype(F32)).astype(BF16)
        return carry

    lax.fori_loop(0, tg // c, chunk, 0)


def _gla_call(p, lg, pm, lgm, norm_w, batch, tg):
    rows = p.shape[0]
    nblk = rows // batch // tg
    m_all, mask_all, m_meta = _gla_constants()
    kw, vw = GLA_HEADS * GLA_DK, GLA_HEADS * GLA_DV
    row = lambda b, s: b * nblk + s
    const2 = lambda b, s: (0, 0)
    kernel = functools.partial(_gla_kernel, tg=tg)
    return pl.pallas_call(
        kernel,
        grid=(batch, nblk),
        in_specs=[
            pl.BlockSpec((tg, kw), lambda b, s: (row(b, s), P_GQ * LANES // kw)),
            pl.BlockSpec((tg, kw), lambda b, s: (row(b, s), P_GK * LANES // kw)),
            pl.BlockSpec((tg, vw), lambda b, s: (row(b, s), P_GV * LANES // vw)),
            pl.BlockSpec((tg, kw), lambda b, s: (row(b, s), 0)),
            pl.BlockSpec((tg, vw), lambda b, s: (row(b, s), P_GZ * LANES // vw)),
            pl.BlockSpec((N_META, kw), lambda b, s: (0, P_GK * LANES // kw)),
            pl.BlockSpec((N_META, vw), lambda b, s: (0, P_GV * LANES // vw)),
            pl.BlockSpec((N_META, kw), const2),
            pl.BlockSpec(m_all.shape, const2),
            pl.BlockSpec(mask_all.shape, lambda b, s: (0, 0, 0)),
            pl.BlockSpec(m_meta.shape, const2),
            pl.BlockSpec((1, GLA_DV), const2),
        ],
        out_specs=pl.BlockSpec((tg, vw), lambda b, s: (row(b, s), 0)),
        out_shape=jax.ShapeDtypeStruct((rows, vw), BF16),
        scratch_shapes=[pltpu.VMEM((GLA_HEADS, GLA_DV, GLA_DK), F32)],
        compiler_params=_params(2),
        name="gla",
    )(p, p, p, lg, p, pm, pm, lgm, jnp.asarray(m_all, BF16), jnp.asarray(mask_all, F32),
      jnp.asarray(m_meta, BF16), norm_w)


def _out_kernel(oa_ref, ob_ref, ga_ref, gb_ref, x_ref, wa_ref, wb_ref, wo_ref, fw_ref, o_ref):
    ya = _nn(oa_ref[...], wa_ref[...])
    yb = _nn(ob_ref[...], wb_ref[...])
    merged = ga_ref[...].astype(F32) * ya + gb_ref[...].astype(F32) * yb
    hid = x_ref[...] + _nn(merged.astype(BF16), wo_ref[...])
    ms = jnp.mean(hid * hid, axis=-1, keepdims=True)
    o_ref[...] = hid * lax.rsqrt(ms + NORM_EPS) * fw_ref[...]


def _out_call(oa, ob, p, x2, wa, wb, wo, fw, tm):
    rows, d = x2.shape
    rowblk = lambda i: (i, 0)
    const = lambda i: (0, 0)
    return pl.pallas_call(
        _out_kernel,
        grid=(rows // tm,),
        in_specs=[
            pl.BlockSpec((tm, d), rowblk),
            pl.BlockSpec((tm, d), rowblk),
            pl.BlockSpec((tm, d), lambda i: (i, P_GA * LANES // d)),
            pl.BlockSpec((tm, d), lambda i: (i, P_GB * LANES // d)),
            pl.BlockSpec((tm, d), rowblk),
            pl.BlockSpec((d, d), const),
            pl.BlockSpec((d, d), const),
            pl.BlockSpec((d, d), const),
            pl.BlockSpec((1, d), const),
        ],
        out_specs=pl.BlockSpec((tm, d), rowblk),
        out_shape=jax.ShapeDtypeStruct((rows, d), F32),
        compiler_params=_params(1),
        name="out",
    )(oa, ob, p, p, x2, wa, wb, wo, fw)


def _head_lane_layout():
    half_r = DA_ROT_DIM // 2
    n_plain = (DA_HEAD_DIM - DA_ROT_DIM) // 2
    runs, is_map1 = [], []
    for half in range(2):
        rot0 = half * half_r
        plain0 = DA_ROT_DIM + half * n_plain
        for which, d0, n in ((0, rot0, half_r), (1, rot0, half_r),
                             (0, plain0, n_plain), (1, plain0, n_plain)):
            runs.append((which, d0, n))
            is_map1 += [which == 0] * n
    return runs, np.array(is_map1)


def _permute_head_columns(w_cols, runs):
    d = w_cols.shape[0]
    w4 = w_cols.reshape(d, DA_HEADS, 2, DA_HEAD_DIM)
    pieces = [w4[:, :, which, d0:d0 + n] for which, d0, n in runs]
    return jnp.concatenate(pieces, axis=-1).reshape(d, DA_HEADS * 2 * DA_HEAD_DIM)


def _rope_tables(pos, scale):
    half_r = DA_ROT_DIM // 2
    lane = np.arange(LANES) % (LANES // 2)
    rotary = lane < DA_ROT_DIM
    inv_freq = ROPE_THETA ** (-jnp.arange(half_r, dtype=F32) / half_r)
    inv_lane = jnp.where(jnp.asarray(rotary), inv_freq[lane % half_r], 0.0)
    sign = np.where(rotary, np.where(np.arange(LANES) < LANES // 2, -1.0, 1.0), 0.0)
    ang = pos.astype(F32)[:, None] * inv_lane[None, :]
    return jnp.cos(ang) * scale, jnp.sin(ang) * jnp.asarray(sign * scale, F32)[None, :]


def kernel(x, meta_tokens, norm_w, w_in, lam_q1, lam_k1, lam_q2, lam_k2, da_subln_w, gla_gate_w2,
           gla_gate_b, gla_norm_w, w_branch_a, w_branch_b, w_out, final_norm_w):
    batch, seq, d = x.shape
    depth = norm_w.shape[0]
    assert depth == 1
    layer = 0
    lam_init = 0.8 - 0.6 * math.exp(-0.3 * layer)
    rows = batch * seq
    x2 = x.reshape(rows, d)

    w_all = w_in[layer].astype(BF16)
    c = np.cumsum([0, 1024, 1024, 1024, 1024, 512, 512, 1024, 1024, GLA_GATE_RANK, 1024, 1024])
    sl = lambda a: w_all[:, c[a]:c[a + 1]]
    runs, lane_is_map1 = _head_lane_layout()
    w_qk = jnp.concatenate([_permute_head_columns(sl(0), runs),
                            _permute_head_columns(sl(1), runs)], axis=1)
    w_mid = w_all[:, c[3]:c[8]]
    w_gate = w_all[:, c[9]:c[11]]
    wvt = jnp.pad(sl(2).T.reshape(DA_HEADS, DA_V_DIM, d),
                  ((0, 0), (0, VT_ROWS - DA_V_DIM), (0, 0))).reshape(DA_HEADS * VT_ROWS, d)
    ones_col = jnp.asarray((np.arange(DA_HEADS * VT_ROWS) % VT_ROWS == DA_V_DIM)[:, None], F32)
    wlr = jnp.pad(sl(8), ((0, 0), (0, LANES - GLA_GATE_RANK)))
    w2 = jnp.pad(gla_gate_w2[layer], ((0, LANES - GLA_GATE_RANK), (0, 0))).astype(BF16)
    gate_b = gla_gate_b[layer][None, :]
    nw = norm_w[layer][None, :]

    q_scale = DA_HEAD_DIM ** -0.5 * math.log2(math.e)
    pos_meta = jnp.arange(N_META, dtype=jnp.int32)
    pos_real = jnp.arange(N_META, N_META + seq, dtype=jnp.int32)
    tabs_meta = _rope_tables(pos_meta, q_scale) + _rope_tables(pos_meta, 1.0)
    tabs_real = _rope_tables(pos_real, q_scale) + _rope_tables(pos_real, 1.0)
    map1_lanes = jnp.asarray(lane_is_map1[None, :], F32)

    tm_proj = min(1024, seq)
    tq = min(512, seq)
    tg = min(512, seq)
    tm_out = min(512, seq)

    pm, um = _proj_call(meta_tokens.astype(F32), nw, w_qk, w_mid, w_gate, tabs_meta, tm=N_META)
    vtm = _vt_call(wvt, ones_col, um, 1, N_META)[0, 0]
    lgm = _lg_call(um, wlr, w2, gate_b, N_META)

    p, u = _proj_call(x2, nw, w_qk, w_mid, w_gate, tabs_real, tm=tm_proj)
    vt = _vt_call(wvt, ones_col, u, batch, tq)
    lg = _lg_call(u, wlr, w2, gate_b, tm_proj)

    lam_vecs = [v[layer][None, :] for v in (lam_q1, lam_k1, lam_q2, lam_k2)]
    oa = _attn_call(p, vt, pm, vtm, map1_lanes, lam_vecs, da_subln_w[layer][:, None], batch, tq,
                    lam_init)
    ob = _gla_call(p, lg, pm, lgm, gla_norm_w[layer][None, :], batch, tg)

    out = _out_call(oa, ob, p, x2, w_branch_a[layer].astype(BF16), w_branch_b[layer].astype(BF16),
                    w_out[layer].astype(BF16), final_norm_w[None, :], tm_out)
    return out.reshape(batch, seq, d)
```

```python
import functools
import math

import numpy as np
import jax
import jax.numpy as jnp
from jax import lax
from jax.experimental import pallas as pl
from jax.experimental.pallas import tpu as pltpu

F32 = jnp.float32
BF16 = jnp.bfloat16

N_META = 16
ROPE_THETA = 500000.0
NORM_EPS = 1e-5

DA_HEADS = 8
DA_HEAD_DIM = 64
DA_V_DIM = 128
VT_ROWS = DA_V_DIM + 16
DA_ROT_DIM = 16
GLA_HEADS = 4
GLA_DK = 128
GLA_DV = 256
GLA_GATE_RANK = 16
GLA_GATE_NORMALIZER = 16.0
GLA_CHUNK = 128

LANES = 128
VMEM_LIMIT = 48 * 1024 * 1024
ATTN_VMEM_LIMIT = 56 * 1024 * 1024

P_Q, P_K, P_Z, P_GQ, P_GK, P_GV, P_GZ, P_GA, P_GB = 0, 8, 16, 24, 28, 32, 40, 48, 56
P_COLS = 64 * LANES
PROJ_TN = 1024
PROJ_RC = 256

NEG_BIG = -1e30


def _nt(a, b):
    return lax.dot_general(a, b, (((1,), (1,)), ((), ())), preferred_element_type=F32)


def _tn(a, b):
    return lax.dot_general(a, b, (((0,), (0,)), ((), ())), preferred_element_type=F32)


def _nn(a, b):
    return jnp.dot(a, b, preferred_element_type=F32)


def _params(n_axes, vmem=VMEM_LIMIT):
    return pltpu.CompilerParams(dimension_semantics=("arbitrary",) * n_axes,
                                vmem_limit_bytes=vmem)


_P_UNIT = 512
_P_UNIT_KINDS = ("rope_q", "rope_q", "rope_k", "rope_k", "silu", "silu", "scale_gq", "id",
                 "id", "id", "silu", "silu", "sigmoid", "sigmoid", "sigmoid", "sigmoid")


def _proj_kernel(x_ref, nw_ref, wqk_ref, wmid_ref, wgate_ref, cq_ref, sq_ref, ck_ref, sk_ref,
                 p_ref, u_ref, u_scr, *, rc, tn, n_qk, n_mid):
    j = pl.program_id(1)
    tm = x_ref.shape[0]

    @pl.when(j == 0)
    def _():
        x = x_ref[...]
        ms = jnp.mean(x * x, axis=-1, keepdims=True)
        u = (x * lax.rsqrt(ms + NORM_EPS) * nw_ref[...]).astype(BF16)
        u_scr[...] = u
        u_ref[...] = u

    def rope(t, c, s):
        outs = [t[:, g:g + LANES] * c + pltpu.roll(t[:, g:g + LANES], LANES // 2, 1) * s
                for g in range(0, t.shape[1], LANES)]
        return jnp.concatenate(outs, axis=1)

    def unit_epilogue(kind, t, rows):
        if kind == "rope_q":
            return rope(t, cq_ref[rows, :], sq_ref[rows, :])
        if kind == "rope_k":
            return rope(t, ck_ref[rows, :], sk_ref[rows, :])
        if kind == "silu":
            return t * jax.nn.sigmoid(t)
        if kind == "sigmoid":
            return jax.nn.sigmoid(t)
        if kind == "scale_gq":
            return t * (GLA_DK ** -0.5)
        return t

    def run(w_ref, kinds):
        for r in range(tm // rc):
            rows = slice(r * rc, (r + 1) * rc)
            y = _nn(u_scr[rows, :], w_ref[...])
            outs = [unit_epilogue(kind, y[:, n * _P_UNIT:(n + 1) * _P_UNIT], rows)
                    for n, kind in enumerate(kinds)]
            p_ref[rows, :] = jnp.concatenate(outs, axis=1).astype(BF16)

    units = tn // _P_UNIT
    n_tiles = P_COLS // tn
    tile_kinds = [_P_UNIT_KINDS[t * units:(t + 1) * units] for t in range(n_tiles)]
    source = lambda t: wqk_ref if t < n_qk else (wmid_ref if t < n_qk + n_mid else wgate_ref)
    for kinds in sorted(set(tile_kinds)):
        tiles = [t for t in range(n_tiles) if tile_kinds[t] == kinds]
        w_ref = source(tiles[0])
        assert all(source(t) is w_ref for t in tiles)
        cond = functools.reduce(lambda a, b: a | b, [j == t for t in tiles])
        pl.when(cond)(functools.partial(run, w_ref, kinds))


def _proj_call(x2, norm_w, w_qk, w_mid, w_gate, tabs, tm):
    rows, d = x2.shape
    tn = PROJ_TN
    n_pos_blocks = tabs[0].shape[0] // tm
    grid = (rows // tm, P_COLS // tn)
    n_qk, n_mid, n_gate = w_qk.shape[1] // tn, w_mid.shape[1] // tn, w_gate.shape[1] // tn
    assert n_qk + n_mid + n_gate == grid[1]
    tab_spec = pl.BlockSpec((tm, LANES), lambda i, j: (i % n_pos_blocks, 0))
    kernel = functools.partial(_proj_kernel, rc=min(PROJ_RC, tm), tn=tn, n_qk=n_qk, n_mid=n_mid)
    return pl.pallas_call(
        kernel,
        grid=grid,
        in_specs=[
            pl.BlockSpec((tm, d), lambda i, j: (i, 0)),
            pl.BlockSpec((1, d), lambda i, j: (0, 0)),
            pl.BlockSpec((d, tn), lambda i, j: (0, jnp.minimum(j, n_qk - 1))),
            pl.BlockSpec((d, tn), lambda i, j: (0, jnp.clip(j - n_qk, 0, n_mid - 1))),
            pl.BlockSpec((d, tn), lambda i, j: (0, jnp.clip(j - n_qk - n_mid, 0, n_gate - 1))),
            tab_spec, tab_spec, tab_spec, tab_spec,
        ],
        out_specs=[
            pl.BlockSpec((tm, tn), lambda i, j: (i, j)),
            pl.BlockSpec((tm, d), lambda i, j: (i, 0)),
        ],
        out_shape=[
            jax.ShapeDtypeStruct((rows, P_COLS), BF16),
            jax.ShapeDtypeStruct((rows, d), BF16),
        ],
        scratch_shapes=[pltpu.VMEM((tm, d), BF16)],
        compiler_params=_params(2),
        name="proj",
    )(x2, norm_w, w_qk, w_mid, w_gate, *tabs)


def _vt_kernel(wvt_ref, ones_ref, u_ref, o_ref):
    o_ref[0, 0] = (_nt(wvt_ref[...], u_ref[...]) + ones_ref[...]).astype(BF16)


def _vt_call(wvt, ones_col, u, batch, tk):
    rows, d = u.shape
    n = wvt.shape[0]
    nblk = rows // batch // tk
    return pl.pallas_call(
        _vt_kernel,
        grid=(batch, nblk),
        in_specs=[
            pl.BlockSpec((n, d), lambda b, s: (0, 0)),
            pl.BlockSpec((n, 1), lambda b, s: (0, 0)),
            pl.BlockSpec((tk, d), lambda b, s: (b * nblk + s, 0)),
        ],
        out_specs=pl.BlockSpec((1, 1, n, tk), lambda b, s: (b, s, 0, 0)),
        out_shape=jax.ShapeDtypeStruct((batch, nblk, n, tk), BF16),
        compiler_params=_params(2),
        name="vt",
    )(wvt, ones_col, u)


def _lg_kernel(u_ref, wlr_ref, w2_ref, b_ref, o_ref):
    g_lr = _nn(u_ref[...], wlr_ref[...])
    gk = _nn(g_lr.astype(BF16), w2_ref[...]) + b_ref[...]
    log_sig = jnp.minimum(gk, 0.0) - jnp.log1p(jnp.exp(-jnp.abs(gk)))
    o_ref[...] = log_sig * (math.log2(math.e) / GLA_GATE_NORMALIZER)


def _lg_call(u, wlr, w2, bias, tm):
    rows, d = u.shape
    n = w2.shape[1]
    return pl.pallas_call(
        _lg_kernel,
        grid=(rows // tm,),
        in_specs=[
            pl.BlockSpec((tm, d), lambda i: (i, 0)),
            pl.BlockSpec(wlr.shape, lambda i: (0, 0)),
            pl.BlockSpec(w2.shape, lambda i: (0, 0)),
            pl.BlockSpec((1, n), lambda i: (0, 0)),
        ],
        out_specs=pl.BlockSpec((tm, n), lambda i: (i, 0)),
        out_shape=jax.ShapeDtypeStruct((rows, n), F32),
        compiler_params=_params(1),
        name="lg",
    )(u, wlr, w2, bias)


ATTN_GROUP = 2


def _snake_tile(s, nq):
    h = s // nq
    a = s - h * nq
    return h, jnp.where(lax.rem(h, 2) == 0, a, nq - 1 - a)


def _attn_kernel(q_ref, k_ref, vt_ref, km_ref, vtm_ref, z_ref, map1_ref, bias_ref, lq1_ref,
                 lk1_ref, lq2_ref, lk2_ref, sw_ref, o_ref, qq_scr, s0_scr, s1_scr, sm_scr, m_scr,
                 mp_scr, acc_scr, lam_scr, *, tq, nq, n_heads, lam_init):
    t = pl.program_id(1)
    n_pos = n_heads * nq
    w = 2 * tq
    grp = ATTN_GROUP
    cur = lax.rem(t, 2)
    prv = 1 - cur
    _, tile1 = _snake_tile(jnp.minimum(t, n_pos - 1), nq)
    _, tile2 = _snake_tile(jnp.clip(t - 1, 0, n_pos - 1), nq)
    cnt1 = jnp.where(t < n_pos, tile1 + 1, 0)
    cnt2 = jnp.where((t >= 1) & (t <= n_pos), tile2 + 1, 0)
    n_both = jnp.where(cnt1 > 0, jnp.minimum(tile1, cnt2), 0)
    diag_paired = (cnt1 > 0) & (cnt2 > n_both)
    diag_alone = (cnt1 > 0) & (cnt2 <= n_both)
    start2 = n_both + jnp.where(diag_paired, 1, 0)
    rest2 = cnt2 - start2

    def fold_max(x):
        return jnp.max(x.reshape(x.shape[0] // 8, 8, w), axis=0)

    def p1_start():
        qb = q_ref[...]
        in_map1 = jnp.broadcast_to(map1_ref[...], qb.shape) > 0.5
        zero = jnp.zeros_like(qb)
        qq_scr[0:tq, :] = jnp.where(in_map1, qb, zero)
        qq_scr[tq:w, :] = jnp.where(in_map1, zero, qb)
        s_meta = _nt(km_ref[...], qq_scr[...])
        sm_scr[cur] = s_meta
        mp_scr[...] = fold_max(s_meta)

    def p1_blocks(s_p1, j0, n, diagonal=False):
        for u in range(n):
            j = j0 + u
            kb = k_ref[pl.ds(pl.multiple_of(j * tq, tq), tq), :]
            s = _nt(kb, qq_scr[...])
            if diagonal:
                s = s + bias_ref[...]
            s_p1[j] = s
            mp_scr[...] = jnp.maximum(mp_scr[...], fold_max(s))

    def p2_probs(s, m):
        return jnp.exp2(s - m).astype(BF16)

    def p2_blocks(s_p2, j0, n, m):
        for u in range(n):
            acc_scr[...] += _nn(vt_ref[0, j0 + u], p2_probs(s_p2[j0 + u], m))

    def p2_meta():
        acc_scr[...] += _nn(vtm_ref[...], p2_probs(sm_scr[prv], m_scr[prv]))

    def finish():
        inv_l = 1.0 / acc_scr[DA_V_DIM:DA_V_DIM + 1, :]
        acc = acc_scr[0:DA_V_DIM, :]
        o = acc[:, 0:tq] * inv_l[:, 0:tq] - lam_scr[...] * (acc[:, tq:w] * inv_l[:, tq:w])
        ms = jnp.mean(o * o, axis=0, keepdims=True)
        y = o * lax.rsqrt(ms + NORM_EPS) * sw_ref[...] * (1.0 - lam_init)
        o_ref[...] = (y.T * z_ref[...].astype(F32)).astype(BF16)

    def grouped(n, body):
        n_grp = n // grp
        lax.fori_loop(0, n_grp, lambda g, c: (body(g * grp, grp), c)[1], 0)
        rem = n - n_grp * grp
        for u in range(grp - 1):
            pl.when(rem > u)(functools.partial(body, n_grp * grp + u, 1))

    def middle(s_p1, s_p2):
        m_prev = m_scr[prv]

        def both(j0, n):
            p1_blocks(s_p1, j0, n)
            p2_blocks(s_p2, j0, n, m_prev)

        def diagonal_with_pass2():
            p1_blocks(s_p1, tile1, 1, diagonal=True)
            p2_blocks(s_p2, n_both, 1, m_prev)

        grouped(n_both, both)
        pl.when(diag_paired)(diagonal_with_pass2)
        pl.when(diag_alone)(functools.partial(p1_blocks, s_p1, tile1, 1, diagonal=True))
        grouped(rest2, lambda j0, n: p2_blocks(s_p2, start2 + j0, n, m_prev))
        m_scr[cur] = jnp.max(mp_scr[...], axis=0, keepdims=True)
        p2_meta()

    @pl.when(t == 0)
    def _():
        m_scr[...] = jnp.zeros_like(m_scr)
        sm_scr[...] = jnp.zeros_like(sm_scr)
        acc_scr[...] = jnp.ones_like(acc_scr)
        lam = (jnp.exp(jnp.sum(lq1_ref[...] * lk1_ref[...], axis=-1, keepdims=True))
               - jnp.exp(jnp.sum(lq2_ref[...] * lk2_ref[...], axis=-1, keepdims=True))
               + lam_init)
        lam_scr[...] = jnp.broadcast_to(lam, lam_scr.shape)

    finish()
    acc_scr[...] = jnp.zeros_like(acc_scr)
    p1_start()

    @pl.when(cur == 0)
    def _():
        middle(s0_scr, s1_scr)

    @pl.when(cur == 1)
    def _():
        middle(s1_scr, s0_scr)


def _attn_call(p, vt, pm, vtm, map1_lanes, lam_vecs, subln_w, batch, tq, lam_init):
    rows = p.shape[0]
    seq = rows // batch
    nq = seq // tq
    n_pos = DA_HEADS * nq
    small = pl.BlockSpec((1, DA_HEAD_DIM), lambda b, t: (0, 0))
    key = np.arange(tq)[:, None]
    qry = np.arange(2 * tq)[None, :] % tq
    bias = jnp.asarray(np.where(key <= qry, 0.0, NEG_BIG), F32)
    kernel = functools.partial(_attn_kernel, tq=tq, nq=nq, n_heads=DA_HEADS, lam_init=lam_init)
    pos1 = lambda t: _snake_tile(jnp.minimum(t, n_pos - 1), nq)
    pos2 = lambda t: _snake_tile(jnp.clip(t - 1, 0, n_pos - 1), nq)
    pos3 = lambda t: _snake_tile(jnp.clip(t - 2, 0, n_pos - 1), nq)
    return pl.pallas_call(
        kernel,
        grid=(batch, n_pos + 2),
        in_specs=[
            pl.BlockSpec((tq, LANES), lambda b, t: (b * nq + pos1(t)[1], P_Q + pos1(t)[0])),
            pl.BlockSpec((seq, LANES), lambda b, t: (b, P_K + pos1(t)[0])),
            pl.BlockSpec((1, nq, VT_ROWS, tq), lambda b, t: (b, 0, pos2(t)[0], 0)),
            pl.BlockSpec((N_META, LANES), lambda b, t: (0, P_K + pos1(t)[0])),
            pl.BlockSpec((VT_ROWS, N_META), lambda b, t: (pos2(t)[0], 0)),
            pl.BlockSpec((tq, LANES), lambda b, t: (b * nq + pos3(t)[1], P_Z + pos3(t)[0])),
            pl.BlockSpec((1, LANES), lambda b, t: (0, 0)),
            pl.BlockSpec((tq, 2 * tq), lambda b, t: (0, 0)),
            small, small, small, small,
            pl.BlockSpec((DA_V_DIM, 1), lambda b, t: (0, 0)),
        ],
        out_specs=pl.BlockSpec((tq, LANES), lambda b, t: (b * nq + pos3(t)[1], pos3(t)[0])),
        out_shape=jax.ShapeDtypeStruct((rows, DA_HEADS * DA_V_DIM), BF16),
        scratch_shapes=[
            pltpu.VMEM((2 * tq, LANES), BF16),
            pltpu.VMEM((nq, tq, 2 * tq), F32),
            pltpu.VMEM((nq, tq, 2 * tq), F32),
            pltpu.VMEM((2, N_META, 2 * tq), F32),
            pltpu.VMEM((2, 1, 2 * tq), F32),
            pltpu.VMEM((8, 2 * tq), F32),
            pltpu.VMEM((VT_ROWS, 2 * tq), F32),
            pltpu.VMEM((1, tq), F32),
        ],
        compiler_params=_params(2, ATTN_VMEM_LIMIT),
        name="attn",
    )(p, p, vt, pm, vtm, p, map1_lanes, bias, *lam_vecs, subln_w)


_GLA_LEVELS = tuple(GLA_CHUNK >> (i + 1) for i in range(GLA_CHUNK.bit_length() - 1))


def _gla_constants():
    c = GLA_CHUNK
    t = np.arange(c)[:, None]
    i = np.arange(c)[None, :]
    mats, masks = [], []
    for s in _GLA_LEVELS:
        upper = (t % (2 * s)) >= s
        r = (t // (2 * s)) * (2 * s) + s - 1
        mats.append(np.where(upper, (i > r) & (i <= t), (i > t) & (i <= r)))
        masks.append(upper & ~upper.T & ((t // (2 * s)) == (i // (2 * s))))
    mats.append(i <= t)
    mats.append(i > t)
    masks.append(t == i)
    m_all = np.concatenate(mats, axis=0).astype(np.float32)
    mask_all = np.stack(masks, axis=0).astype(np.float32)
    tm = np.arange(N_META)
    m_meta = (tm[None, :] > tm[:, None]).astype(np.float32)
    twice = lambda m: np.concatenate([m, m], axis=1)
    return twice(m_all), mask_all, twice(m_meta)


def _gate_factors(mat2, lg2):
    hi = lg2.astype(BF16)
    lo = (lg2 - hi.astype(F32)).astype(BF16)
    return jnp.exp2(_nn(mat2, jnp.concatenate([hi, lo], axis=0)))


def _gla_kernel(q_ref, k_ref, v_ref, lg_ref, z_ref, km_ref, vm_ref, lgm_ref, mall_ref,
                mask_ref, mmeta_ref, nw_ref, o_ref, st_scr, *, tg):
    sblk = pl.program_id(1)
    c = GLA_CHUNK
    nlev = len(_GLA_LEVELS)

    @pl.when(sblk == 0)
    def _():
        f_meta = _gate_factors(mmeta_ref[...], lgm_ref[...])
        for h in range(GLA_HEADS):
            ksl = slice(h * GLA_DK, (h + 1) * GLA_DK)
            vsl = slice(h * GLA_DV, (h + 1) * GLA_DV)
            kt = (km_ref[:, ksl].astype(F32) * f_meta[:, ksl]).astype(BF16)
            st_scr[h] = _tn(vm_ref[:, vsl], kt)

    def chunk(ci, carry):
        r0 = pl.multiple_of(ci * c, c)
        f_all = _gate_factors(mall_ref[...], lg_ref[pl.ds(r0, c), :])
        for h in range(GLA_HEADS):
            ksl = slice(h * GLA_DK, (h + 1) * GLA_DK)
            vsl = slice(h * GLA_DV, (h + 1) * GLA_DV)
            q = q_ref[pl.ds(r0, c), ksl]
            k = k_ref[pl.ds(r0, c), ksl]
            v = v_ref[pl.ds(r0, c), vsl]
            qf = q.astype(F32)
            kf = k.astype(F32)
            a = jnp.where(mask_ref[nlev] > 0.5, _nt(q, k), 0.0)
            for lv in range(nlev):
                fl = f_all[lv * c:(lv + 1) * c, ksl]
                a = jnp.where(mask_ref[lv] > 0.5,
                              _nt((qf * fl).astype(BF16), (kf * fl).astype(BF16)), a)
            fb = f_all[nlev * c:(nlev + 1) * c, ksl]
            fk = f_all[(nlev + 1) * c:(nlev + 2) * c, ksl]
            st = st_scr[h]
            o = _nn(a.astype(BF16), v) + _nt((qf * fb).astype(BF16), st.astype(BF16))
            st_scr[h] = st * fb[c - 1:c, :] + _tn(v, (kf * fk).astype(BF16))
            ms = jnp.mean(o * o, axis=-1, keepdims=True)
            y = o * lax.rsqrt(ms + NORM_EPS) * nw_ref[...]
            o_ref[pl.ds(r0, c), vsl] = (y * z_ref[pl.ds(r0, c), vsl].astype(F32)).astype(BF16)
        return carry

    lax.fori_loop(0, tg // c, chunk, 0)


def _gla_call(p, lg, pm, lgm, norm_w, batch, tg):
    rows = p.shape[0]
    nblk = rows // batch // tg
    m_all, mask_all, m_meta = _gla_constants()
    kw, vw = GLA_HEADS * GLA_DK, GLA_HEADS * GLA_DV
    row = lambda b, s: b * nblk + s
    const2 = lambda b, s: (0, 0)
    kernel = functools.partial(_gla_kernel, tg=tg)
    return pl.pallas_call(
        kernel,
        grid=(batch, nblk),
        in_specs=[
            pl.BlockSpec((tg, kw), lambda b, s: (row(b, s), P_GQ * LANES // kw)),
            pl.BlockSpec((tg, kw), lambda b, s: (row(b, s), P_GK * LANES // kw)),
            pl.BlockSpec((tg, vw), lambda b, s: (row(b, s), P_GV * LANES // vw)),
            pl.BlockSpec((tg, kw), lambda b, s: (row(b, s), 0)),
            pl.BlockSpec((tg, vw), lambda b, s: (row(b, s), P_GZ * LANES // vw)),
            pl.BlockSpec((N_META, kw), lambda b, s: (0, P_GK * LANES // kw)),
            pl.BlockSpec((N_META, vw), lambda b, s: (0, P_GV * LANES // vw)),
            pl.BlockSpec((N_META, kw), const2),
            pl.BlockSpec(m_all.shape, const2),
            pl.BlockSpec(mask_all.shape, lambda b, s: (0, 0, 0)),
            pl.BlockSpec(m_meta.shape, const2),
            pl.BlockSpec((1, GLA_DV), const2),
        ],
        out_specs=pl.BlockSpec((tg, vw), lambda b, s: (row(b, s), 0)),
        out_shape=jax.ShapeDtypeStruct((rows, vw), BF16),
        scratch_shapes=[pltpu.VMEM((GLA_HEADS, GLA_DV, GLA_DK), F32)],
        compiler_params=_params(2),
        name="gla",
    )(p, p, p, lg, p, pm, pm, lgm, jnp.asarray(m_all, BF16), jnp.asarray(mask_all, F32),
      jnp.asarray(m_meta, BF16), norm_w)


def _out_kernel(oa_ref, ob_ref, ga_ref, gb_ref, x_ref, wa_ref, wb_ref, wo_ref, fw_ref, o_ref):
    ya = _nn(oa_ref[...], wa_ref[...])
    yb = _nn(ob_ref[...], wb_ref[...])
    merged = ga_ref[...].astype(F32) * ya + gb_ref[...].astype(F32) * yb
    hid = x_ref[...] + _nn(merged.astype(BF16), wo_ref[...])
    ms = jnp.mean(hid * hid, axis=-1, keepdims=True)
    o_ref[...] = hid * lax.rsqrt(ms + NORM_EPS) * fw_ref[...]


def _out_call(oa, ob, p, x2, wa, wb, wo, fw, tm):
    rows, d = x2.shape
    rowblk = lambda i: (i, 0)
    const = lambda i: (0, 0)
    return pl.pallas_call(
        _out_kernel,
        grid=(rows // tm,),
        in_specs=[
            pl.BlockSpec((tm, d), rowblk),
            pl.BlockSpec((tm, d), rowblk),
            pl.BlockSpec((tm, d), lambda i: (i, P_GA * LANES // d)),
            pl.BlockSpec((tm, d), lambda i: (i, P_GB * LANES // d)),
            pl.BlockSpec((tm, d), rowblk),
            pl.BlockSpec((d, d), const),
            pl.BlockSpec((d, d), const),
            pl.BlockSpec((d, d), const),
            pl.BlockSpec((1, d), const),
        ],
        out_specs=pl.BlockSpec((tm, d), rowblk),
        out_shape=jax.ShapeDtypeStruct((rows, d), F32),
        compiler_params=_params(1),
        name="out",
    )(oa, ob, p, p, x2, wa, wb, wo, fw)


def _head_lane_layout():
    half_r = DA_ROT_DIM // 2
    n_plain = (DA_HEAD_DIM - DA_ROT_DIM) // 2
    runs, is_map1 = [], []
    for half in range(2):
        rot0 = half * half_r
        plain0 = DA_ROT_DIM + half * n_plain
        for which, d0, n in ((0, rot0, half_r), (1, rot0, half_r),
                             (0, plain0, n_plain), (1, plain0, n_plain)):
            runs.append((which, d0, n))
            is_map1 += [which == 0] * n
    return runs, np.array(is_map1)


def _permute_head_columns(w_cols, runs):
    d = w_cols.shape[0]
    w4 = w_cols.reshape(d, DA_HEADS, 2, DA_HEAD_DIM)
    pieces = [w4[:, :, which, d0:d0 + n] for which, d0, n in runs]
    return jnp.concatenate(pieces, axis=-1).reshape(d, DA_HEADS * 2 * DA_HEAD_DIM)


def _rope_tables(pos, scale):
    half_r = DA_ROT_DIM // 2
    lane = np.arange(LANES) % (LANES // 2)
    rotary = lane < DA_ROT_DIM
    inv_freq = ROPE_THETA ** (-jnp.arange(half_r, dtype=F32) / half_r)
    inv_lane = jnp.where(jnp.asarray(rotary), inv_freq[lane % half_r], 0.0)
    sign = np.where(rotary, np.where(np.arange(LANES) < LANES // 2, -1.0, 1.0), 0.0)
    ang = pos.astype(F32)[:, None] * inv_lane[None, :]
    return jnp.cos(ang) * scale, jnp.sin(ang) * jnp.asarray(sign * scale, F32)[None, :]


def kernel(x, meta_tokens, norm_w, w_in, lam_q1, lam_k1, lam_q2, lam_k2, da_subln_w, gla_gate_w2,
           gla_gate_b, gla_norm_w, w_branch_a, w_branch_b, w_out, final_norm_w):
    batch, seq, d = x.shape
    depth = norm_w.shape[0]
    assert depth == 1
    layer = 0
    lam_init = 0.8 - 0.6 * math.exp(-0.3 * layer)
    rows = batch * seq
    x2 = x.reshape(rows, d)

    w = w_in[layer]
    c = np.cumsum([0, 1024, 1024, 1024, 1024, 512, 512, 1024, 1024, GLA_GATE_RANK, 1024, 1024])
    sl = lambda a: w[:, c[a]:c[a + 1]]
    runs, lane_is_map1 = _head_lane_layout()
    w_qk = jnp.concatenate([_permute_head_columns(sl(0), runs),
                            _permute_head_columns(sl(1), runs)], axis=1).astype(BF16)
    w_mid = w[:, c[3]:c[8]].astype(BF16)
    w_gate = w[:, c[9]:c[11]].astype(BF16)
    wvt = jnp.pad(sl(2).T.reshape(DA_HEADS, DA_V_DIM, d),
                  ((0, 0), (0, VT_ROWS - DA_V_DIM), (0, 0))
                  ).reshape(DA_HEADS * VT_ROWS, d).astype(BF16)
    ones_col = jnp.asarray((np.arange(DA_HEADS * VT_ROWS) % VT_ROWS == DA_V_DIM)[:, None], F32)
    wlr = jnp.pad(sl(8), ((0, 0), (0, LANES - GLA_GATE_RANK))).astype(BF16)
    w2 = jnp.pad(gla_gate_w2[layer], ((0, LANES - GLA_GATE_RANK), (0, 0))).astype(BF16)
    gate_b = gla_gate_b[layer][None, :]
    nw = norm_w[layer][None, :]

    q_scale = DA_HEAD_DIM ** -0.5 * math.log2(math.e)
    pos_meta = jnp.arange(N_META, dtype=jnp.int32)
    pos_real = jnp.arange(N_META, N_META + seq, dtype=jnp.int32)
    tabs_meta = _rope_tables(pos_meta, q_scale) + _rope_tables(pos_meta, 1.0)
    tabs_real = _rope_tables(pos_real, q_scale) + _rope_tables(pos_real, 1.0)
    map1_lanes = jnp.asarray(lane_is_map1[None, :], F32)

    tm_proj = min(1024, seq)
    tq = min(512, seq)
    tg = min(512, seq)
    tm_out = min(512, seq)

    pm, um = _proj_call(meta_tokens.astype(F32), nw, w_qk, w_mid, w_gate, tabs_meta, tm=N_META)
    vtm = _vt_call(wvt, ones_col, um, 1, N_META)[0, 0]
    lgm = _lg_call(um, wlr, w2, gate_b, N_META)

    p, u = _proj_call(x2, nw, w_qk, w_mid, w_gate, tabs_real, tm=tm_proj)
    vt = _vt_call(wvt, ones_col, u, batch, tq)
    lg = _lg_call(u, wlr, w2, gate_b, tm_proj)

    lam_vecs = [v[layer][None, :] for v in (lam_q1, lam_k1, lam_q2, lam_k2)]
    oa = _attn_call(p, vt, pm, vtm, map1_lanes, lam_vecs, da_subln_w[layer][:, None], batch, tq,
                    lam_init)
    ob = _gla_call(p, lg, pm, lgm, gla_norm_w[layer][None, :], batch, tg)

    out = _out_call(oa, ob, p, x2, w_branch_a[layer].astype(BF16), w_branch_b[layer].astype(BF16),
                    w_out[layer].astype(BF16), final_norm_w[None, :], tm_out)
    return out.reshape(batch, seq, d)
```
